```python
import jax, jax.numpy as jnp
from jax import lax
import numpy as np

D_MODEL = 1024
BATCH = 8
SEQ = 2048
DEPTH = 4

CTX_LEN = 256
GRID_W = 64
Q_BLOCK = 128
ROPE_THETA = 10000.0
EPS = 1e-6

F_GROUPS = 4
F_GROUP_DIM = 128
F_WIDTH = F_GROUPS * F_GROUP_DIM
MLA_HEADS = 8
MLA_Q_RANK = 256
MLA_KV_RANK = 256
MLA_NOPE_DIM = 64
MLA_ROPE_DIM = 32
MLA_QK_DIM = MLA_NOPE_DIM + MLA_ROPE_DIM
MLA_V_DIM = 64
MLA_WIDTH = MLA_HEADS * MLA_V_DIM
GQA_HEADS = 8
GQA_KV_HEADS = 2
GQA_GROUP = GQA_HEADS // GQA_KV_HEADS
GQA_HEAD_DIM = 64
GQA_WIDTH = GQA_HEADS * GQA_HEAD_DIM
GQA_KV_WIDTH = GQA_KV_HEADS * GQA_HEAD_DIM

N_BRANCHES = 3
D_FF = 4 * D_MODEL
N_MOD = 6

KV_COLS = MLA_KV_RANK + MLA_ROPE_DIM + 2 * GQA_KV_WIDTH
KV_SPLITS = (MLA_KV_RANK, MLA_KV_RANK + MLA_ROPE_DIM, MLA_KV_RANK + MLA_ROPE_DIM + GQA_KV_WIDTH)
REST_SPLITS = (F_WIDTH, F_WIDTH + MLA_Q_RANK, F_WIDTH + MLA_Q_RANK + GQA_WIDTH)
IN_COLS = KV_COLS + F_WIDTH + MLA_Q_RANK + GQA_WIDTH + N_BRANCHES * D_MODEL

kernel_name = "hybrid_fourier_mla_gqa_dit_prefix"


def layer_norm(x, g=None, b=None):
    xf = x.astype(jnp.float32)
    mu = xf.mean(-1, keepdims=True)
    var = jnp.square(xf - mu).mean(-1, keepdims=True)
    y = (xf - mu) * lax.rsqrt(var + EPS)
    if g is not None:
        y = y * g.astype(jnp.float32) + b.astype(jnp.float32)
    return y.astype(x.dtype)


def rms_norm(x, g):
    xf = x.astype(jnp.float32)
    y = xf * lax.rsqrt(jnp.square(xf).mean(-1, keepdims=True) + EPS)
    return (y * g.astype(jnp.float32)).astype(x.dtype)


def modulate(x, shift, scale):
    return layer_norm(x) * (1 + scale) + shift


def rope_angles(rows, cols, d_rot):
    n = d_rot // 4
    freqs = ROPE_THETA ** (-jnp.arange(n, dtype=jnp.float32) / n)
    ang = jnp.concatenate([rows[:, None] * freqs, cols[:, None] * freqs], axis=-1)
    return jnp.cos(ang), jnp.sin(ang)


def apply_rope(x, cos, sin):
    x1, x2 = jnp.split(x, 2, axis=-1)
    cos = cos[:, None, :].astype(x.dtype)
    sin = sin[:, None, :].astype(x.dtype)
    return jnp.concatenate([x1 * cos - x2 * sin, x1 * sin + x2 * cos], axis=-1)


def attend(q, k, v):
    B, S, Hk, G, dk = q.shape
    nb = S // Q_BLOCK
    qb = jnp.moveaxis(q.reshape(B, nb, Q_BLOCK, Hk, G, dk), 1, 0)

    def block(qi):
        s = jnp.einsum("bqkgd,btkd->bkgqt", qi, k).astype(jnp.float32)
        w = jax.nn.softmax(s, axis=-1).astype(v.dtype)
        return jnp.einsum("bkgqt,btkd->bqkgd", w, v)

    o = lax.map(block, qb)
    return jnp.moveaxis(o, 0, 1).reshape(B, S, Hk * G * v.shape[-1])


def fourier_mix(u):
    B, S, _ = u.shape
    ug = u.reshape(B, S, F_GROUPS, F_GROUP_DIM).astype(jnp.float32)
    f = jnp.fft.fft2(ug, axes=(1, 3), norm="ortho").real
    return f.reshape(B, S, F_WIDTH).astype(u.dtype)


def kv_parts(p, lw, rope):
    B, T, _ = p.shape
    c_kv, k_r, k_g, v_g = jnp.split(p[..., :KV_COLS], KV_SPLITS, axis=-1)
    c_kv = rms_norm(c_kv, lw["mla_kv_g"])
    k_nope = (c_kv @ lw["w_uk"]).reshape(B, T, MLA_HEADS, MLA_NOPE_DIM)
    v_m = (c_kv @ lw["w_uv"]).reshape(B, T, MLA_HEADS, MLA_V_DIM)
    k_r = k_r[:, :, None, :]
    k_g = rms_norm(k_g.reshape(B, T, GQA_KV_HEADS, GQA_HEAD_DIM), lw["gqa_k_g"])
    v_g = v_g.reshape(B, T, GQA_KV_HEADS, GQA_HEAD_DIM)
    if rope is not None:
        k_r = apply_rope(k_r, *rope[0])
        k_g = apply_rope(k_g, *rope[1])
    k_m = jnp.concatenate([k_nope, jnp.broadcast_to(k_r, (B, T, MLA_HEADS, MLA_ROPE_DIM))], axis=-1)
    return (k_m, v_m, k_g, v_g)


def mixer(p, kv, lw, rope):
    B, S, _ = p.shape
    k_m, v_m, k_g, v_g = kv
    f_in, c_q, q_g, gate_logits = jnp.split(p[..., KV_COLS:], REST_SPLITS, axis=-1)
    y_f = fourier_mix(f_in) @ lw["w_fo"]
    q_m = (rms_norm(c_q, lw["mla_q_g"]) @ lw["w_uq"]).reshape(B, S, MLA_HEADS, MLA_QK_DIM)
    q_nope, q_rope = jnp.split(q_m, [MLA_NOPE_DIM], axis=-1)
    q_g = rms_norm(q_g.reshape(B, S, GQA_HEADS, GQA_HEAD_DIM), lw["gqa_q_g"])
    if rope is not None:
        q_rope = apply_rope(q_rope, *rope[0])
        q_g = apply_rope(q_g, *rope[1])
    q_m = jnp.concatenate([q_nope, q_rope], axis=-1)[:, :, :, None, :] * (MLA_QK_DIM ** -0.5)
    y_m = attend(q_m, k_m, v_m) @ lw["w_mo"]
    q_g = q_g.reshape(B, S, GQA_KV_HEADS, GQA_GROUP, GQA_HEAD_DIM) * (GQA_HEAD_DIM ** -0.5)
    y_g = attend(q_g, k_g, v_g) @ lw["w_go"]
    g_f, g_m, g_g = jnp.split(jax.nn.sigmoid(gate_logits + lw["b_gate"]), N_BRANCHES, axis=-1)
    return (g_f * y_f + g_m * y_m + g_g * y_g) @ lw["w_o"]


def sq_relu_mlp(h, lw):
    return jnp.square(jax.nn.relu(h @ lw["w1"])) @ lw["w2"]


def setup_inputs(seed: int = 0) -> dict:
    key = jax.random.key(seed)
    ks = iter(jax.random.split(key, 32))

    def nrm(shape, scale):
        return jax.random.normal(next(ks), shape, jnp.float32) * scale

    def gain(shape):
        return 1.0 + nrm(shape, 0.02)

    L, D = DEPTH, D_MODEL
    beta = (8.0 * DEPTH) ** -0.25
    return {
        "x": nrm((BATCH, SEQ, D), 1.0),
        "c": nrm((BATCH, D), 1.0),
        "ctx": nrm((BATCH, CTX_LEN, D), 1.0),
        "c_ctx": nrm((D,), 1.0),
        "w_ada": nrm((L, D, N_MOD * D), 0.5 * D ** -0.5),
        "b_ada": nrm((L, N_MOD * D), 0.01),
        "w_in": nrm((L, D, IN_COLS), D ** -0.5),
        "b_gate": nrm((L, N_BRANCHES * D), 0.01),
        "mla_q_g": gain((L, MLA_Q_RANK)),
        "mla_kv_g": gain((L, MLA_KV_RANK)),
        "w_uq": nrm((L, MLA_Q_RANK, MLA_HEADS * MLA_QK_DIM), MLA_Q_RANK ** -0.5),
        "w_uk": nrm((L, MLA_KV_RANK, MLA_HEADS * MLA_NOPE_DIM), MLA_KV_RANK ** -0.5),
        "w_uv": nrm((L, MLA_KV_RANK, MLA_HEADS * MLA_V_DIM), MLA_KV_RANK ** -0.5),
        "gqa_q_g": gain((L, GQA_HEAD_DIM)),
        "gqa_k_g": gain((L, GQA_HEAD_DIM)),
        "w_fo": nrm((L, F_WIDTH, D), F_WIDTH ** -0.5),
        "w_mo": nrm((L, MLA_WIDTH, D), MLA_WIDTH ** -0.5),
        "w_go": nrm((L, GQA_WIDTH, D), GQA_WIDTH ** -0.5),
        "w_o": nrm((L, D, D), beta * D ** -0.5),
        "ln1_g": gain((L, D)),
        "ln1_b": nrm((L, D), 0.02),
        "w1": nrm((L, D, D_FF), D ** -0.5),
        "w2": nrm((L, D_FF, D), beta * D_FF ** -0.5),
        "ln2_g": gain((L, D)),
        "ln2_b": nrm((L, D), 0.02),
    }


def reference(x, c, ctx, c_ctx, w_ada, b_ada, w_in, b_gate, mla_q_g, mla_kv_g, w_uq, w_uk, w_uv,
              gqa_q_g, gqa_k_g, w_fo, w_mo, w_go, w_o, ln1_g, ln1_b, w1, w2, ln2_g, ln2_b):
    B, S, D = x.shape
    ROWS = S // GRID_W
    rows = jnp.repeat(jnp.arange(ROWS), GRID_W).astype(jnp.float32)
    cols = jnp.tile(jnp.arange(GRID_W), ROWS).astype(jnp.float32)
    rope = (rope_angles(rows, cols, MLA_ROPE_DIM), rope_angles(rows, cols, GQA_HEAD_DIM))
    alpha = (2.0 * DEPTH) ** 0.25
    xc = ctx
    for l in range(DEPTH):
        last = l == DEPTH - 1
        lw = {
            "w_in": w_in[l], "b_gate": b_gate[l], "mla_q_g": mla_q_g[l], "mla_kv_g": mla_kv_g[l],
            "w_uq": w_uq[l], "w_uk": w_uk[l], "w_uv": w_uv[l], "gqa_q_g": gqa_q_g[l],
            "gqa_k_g": gqa_k_g[l], "w_fo": w_fo[l], "w_mo": w_mo[l], "w_go": w_go[l], "w_o": w_o[l],
            "w1": w1[l], "w2": w2[l],
        }
        mod_x = (jax.nn.silu(c) @ w_ada[l] + b_ada[l])[:, None, :]
        mod_c = (jax.nn.silu(c_ctx) @ w_ada[l] + b_ada[l])[None, None, :]
        sh1, sc1, g1, sh2, sc2, g2 = jnp.split(mod_x, N_MOD, axis=-1)
        csh1, csc1, cg1, csh2, csc2, cg2 = jnp.split(mod_c, N_MOD, axis=-1)

        h_x = modulate(x, sh1, sc1)
        h_c = modulate(xc, csh1, csc1)
        p_x = h_x @ lw["w_in"]
        p_c = h_c @ (lw["w_in"][:, :KV_COLS] if last else lw["w_in"])
        kv_c = kv_parts(p_c, lw, None)
        kv_x = kv_parts(p_x, lw, rope)
        kv_all = (
            jnp.concatenate([kv_c[0], kv_x[0]], axis=1),
            jnp.concatenate([kv_c[1], kv_x[1]], axis=1),
            jnp.concatenate([kv_c[2], kv_x[2]], axis=1),
            jnp.concatenate([kv_c[3], kv_x[3]], axis=1),
        )
        mix_x = mixer(p_x, kv_all, lw, rope)
        x = layer_norm(alpha * x + g1 * mix_x, ln1_g[l], ln1_b[l])
        x = layer_norm(alpha * x + g2 * sq_relu_mlp(modulate(x, sh2, sc2), lw), ln2_g[l], ln2_b[l])

        if not last:
            mix_c = mixer(p_c, kv_c, lw, None)
            xc = layer_norm(alpha * xc + cg1 * mix_c, ln1_g[l], ln1_b[l])
            xc = layer_norm(alpha * xc + cg2 * sq_relu_mlp(modulate(xc, csh2, csc2), lw), ln2_g[l], ln2_b[l])
    return x
```

```python
import functools
import math

import numpy as np
import jax
import jax.numpy as jnp
from jax import lax
from jax.experimental import pallas as pl
from jax.experimental.pallas import tpu as pltpu

D_MODEL = 1024
BATCH = 8
SEQ = 2048
DEPTH = 4
CTX_LEN = 256
GRID_W = 64
ROPE_THETA = 10000.0
EPS = 1e-6

F_GROUPS = 4
F_GROUP_DIM = 128
F_WIDTH = F_GROUPS * F_GROUP_DIM
MLA_HEADS = 8
MLA_Q_RANK = 256
MLA_KV_RANK = 256
MLA_NOPE_DIM = 64
MLA_ROPE_DIM = 32
MLA_QK_DIM = MLA_NOPE_DIM + MLA_ROPE_DIM
MLA_V_DIM = 64
MLA_WIDTH = MLA_HEADS * MLA_V_DIM
GQA_HEADS = 8
GQA_KV_HEADS = 2
GQA_GROUP = GQA_HEADS // GQA_KV_HEADS
GQA_HEAD_DIM = 64
GQA_WIDTH = GQA_HEADS * GQA_HEAD_DIM
GQA_KV_WIDTH = GQA_KV_HEADS * GQA_HEAD_DIM
N_BRANCHES = 3
D_FF = 4 * D_MODEL
N_MOD = 6
KV_COLS = MLA_KV_RANK + MLA_ROPE_DIM + 2 * GQA_KV_WIDTH

T_ALL = SEQ + CTX_LEN
ALPHA = (2.0 * DEPTH) ** 0.25
LOG2E = math.log2(math.e)

LANES = 128
MXU_DIM = 256
VMEM_LIMIT_BYTES = 56 * 1024 * 1024

MLA_HEAD_PAD = LANES
HEAD_GROUP = 4
MOD_ROWS = 16

A_CKV, A_KG, A_VG, A_F, A_CQ, A_QG, A_KR = 0, 256, 384, 512, 1024, 1280, 1792
A_COLS = 1920
KR_LANE = 64

TM_PROJ = 384
TM_MIX = 384
TM_MLP = 384
TQ = 256
TR = 256
FF_CHUNK = 1024

BF = jnp.bfloat16
F32 = jnp.float32


def _dft_tables(n):
    k = np.arange(n, dtype=np.int64)
    ang = 2.0 * np.pi * ((k[:, None] * k[None, :]) % n).astype(np.float64) / n
    return np.cos(ang).astype(np.float32), (-np.sin(ang)).astype(np.float32)


def _rope_tables():
    t = np.arange(SEQ)
    rows = (t // GRID_W).astype(np.float64)
    cols = (t % GRID_W).astype(np.float64)

    def angles(d_rot):
        n = d_rot // 4
        freqs = ROPE_THETA ** (-np.arange(n, dtype=np.float64) / n)
        return np.concatenate([rows[:, None] * freqs, cols[:, None] * freqs], axis=-1)

    out = np.zeros((6, T_ALL, LANES), np.float64)
    out[0] = 1.0
    out[3] = 1.0
    a = angles(MLA_ROPE_DIM)
    h = MLA_ROPE_DIM // 2
    b0 = MLA_NOPE_DIM
    out[0, :SEQ, b0:b0 + h] = np.cos(a)
    out[0, :SEQ, b0 + h:b0 + 2 * h] = np.cos(a)
    out[1, :SEQ, b0 + h:b0 + 2 * h] = np.sin(a)
    out[2, :SEQ, b0:b0 + h] = -np.sin(a)
    a = angles(GQA_HEAD_DIM)
    h = GQA_HEAD_DIM // 2
    for b0 in (0, GQA_HEAD_DIM):
        out[3, :SEQ, b0:b0 + h] = np.cos(a)
        out[3, :SEQ, b0 + h:b0 + 2 * h] = np.cos(a)
        out[4, :SEQ, b0 + h:b0 + 2 * h] = np.sin(a)
        out[5, :SEQ, b0:b0 + h] = -np.sin(a)
    return out.astype(np.float32)


def _placement_tables():
    e_kr = np.zeros((LANES, MLA_HEADS * MLA_HEAD_PAD), np.float32)
    for hh in range(MLA_HEADS):
        for d in range(MLA_ROPE_DIM):
            e_kr[KR_LANE + d, hh * MLA_HEAD_PAD + MLA_NOPE_DIM + d] = 1.0
    e_rep = np.zeros((GQA_KV_WIDTH, GQA_WIDTH), np.float32)
    for g in range(GQA_KV_HEADS):
        for j in range(GQA_GROUP):
            for d in range(GQA_HEAD_DIM):
                e_rep[g * GQA_HEAD_DIM + d, (g * GQA_GROUP + j) * GQA_HEAD_DIM + d] = 1.0
    return e_kr, e_rep


def _layer_norm(x):
    mu = jnp.mean(x, axis=-1, keepdims=True)
    xc = x - mu
    var = jnp.mean(xc * xc, axis=-1, keepdims=True)
    return xc * lax.rsqrt(var + EPS)


def _rms(x):
    return x * lax.rsqrt(jnp.mean(x * x, axis=-1, keepdims=True) + EPS)


def _is_ctx_rows(tile_idx, tm):
    row = tile_idx * tm + lax.broadcasted_iota(jnp.int32, (tm, 1), 0)
    return row >= SEQ


def _mod_row(is_ctx, modx_ref, modc_ref, i):
    return jnp.where(is_ctx, modc_ref[i:i + 1, :], modx_ref[i:i + 1, :])


def _dot(a, b):
    return jnp.dot(a, b, preferred_element_type=F32)


def _rope(x, cos, sin_fwd, sin_bwd, half):
    return x * cos + pltpu.roll(x, half, 1) * sin_fwd + pltpu.roll(x, LANES - half, 1) * sin_bwd


def _head_rms_pair(x, gain):
    lane = lax.broadcasted_iota(jnp.int32, x.shape, 1)
    first = lane < GQA_HEAD_DIM
    sq = x * x
    s0 = jnp.sum(jnp.where(first, sq, 0.0), axis=-1, keepdims=True)
    s1 = jnp.sum(jnp.where(first, 0.0, sq), axis=-1, keepdims=True)
    ms = jnp.where(first, s0, s1) * (1.0 / GQA_HEAD_DIM)
    return x * lax.rsqrt(ms + EPS) * gain


def _mod_kernel(cc_ref, w_ref, b_ref, o_ref):
    cc = cc_ref[...]
    s = cc * (1.0 / (1.0 + jnp.exp(-cc)))
    o_ref[...] = jnp.dot(s, w_ref[...], preferred_element_type=F32,
                         precision=lax.Precision.HIGHEST) + b_ref[...]


def _modulation(cc, w_ada, b_ada):
    nj = N_MOD
    return pl.pallas_call(
        _mod_kernel,
        grid=(DEPTH, nj),
        in_specs=[
            pl.BlockSpec((MOD_ROWS, D_MODEL), lambda l, j: (0, 0)),
            pl.BlockSpec((None, D_MODEL, D_MODEL), lambda l, j: (l, 0, j)),
            pl.BlockSpec((None, 1, D_MODEL), lambda l, j: (l, 0, j)),
        ],
        out_specs=pl.BlockSpec((None, MOD_ROWS, D_MODEL), lambda l, j: (l, 0, j)),
        out_shape=jax.ShapeDtypeStruct((DEPTH, MOD_ROWS, N_MOD * D_MODEL), F32),
        compiler_params=pltpu.CompilerParams(
            dimension_semantics=("arbitrary", "arbitrary"), vmem_limit_bytes=VMEM_LIMIT_BYTES),
        name="modulation",
    )(cc, w_ada, b_ada.reshape(DEPTH, 1, N_MOD * D_MODEL))


def _proj_kernel(x_ref, modx_ref, modc_ref, rope_ref, wa_ref, wk_ref, wuv_ref, wuq_ref,
                 erep_ref, cs_ref, kvg_ref, qg_ref, gqg_ref, gkg_ref,
                 km_ref, vm_ref, kg_ref, vg_ref, qm_ref, qgo_ref, uc_ref, us_ref):
    tm = x_ref.shape[0]
    is_ctx = _is_ctx_rows(pl.program_id(1), tm)
    shift = _mod_row(is_ctx, modx_ref, modc_ref, 0)
    scale = _mod_row(is_ctx, modx_ref, modc_ref, 1)
    h = (_layer_norm(x_ref[...]) * (1.0 + scale) + shift).astype(BF)
    p = _dot(h, wa_ref[...])

    cos_m, sf_m, sb_m = rope_ref[0], rope_ref[1], rope_ref[2]
    cos_g, sf_g, sb_g = rope_ref[3], rope_ref[4], rope_ref[5]

    ckv = (_rms(p[:, A_CKV:A_CKV + MLA_KV_RANK]) * kvg_ref[...]).astype(BF)
    kr = _rope(p[:, A_KR:A_KR + LANES], cos_m, sf_m, sb_m, MLA_ROPE_DIM // 2).astype(BF)
    km_ref[...] = _dot(jnp.concatenate([ckv, kr], axis=-1), wk_ref[...]).astype(BF)
    vm_ref[...] = _dot(ckv, wuv_ref[...]).astype(BF)

    kg = _rope(_head_rms_pair(p[:, A_KG:A_KG + LANES], gkg_ref[...]), cos_g, sf_g, sb_g, GQA_HEAD_DIM // 2)
    kg_ref[...] = _dot(kg.astype(BF), erep_ref[...]).astype(BF)
    vg_ref[...] = _dot(p[:, A_VG:A_VG + LANES].astype(BF), erep_ref[...]).astype(BF)

    cq = (_rms(p[:, A_CQ:A_CQ + MLA_Q_RANK]) * qg_ref[...]).astype(BF)
    qm = _dot(cq, wuq_ref[...])
    sm = MLA_QK_DIM ** -0.5 * LOG2E
    for hh in range(MLA_HEADS):
        blk = qm[:, hh * LANES:(hh + 1) * LANES]
        qm_ref[:, hh * LANES:(hh + 1) * LANES] = (
            _rope(blk, cos_m, sf_m, sb_m, MLA_ROPE_DIM // 2) * sm).astype(BF)

    sg = GQA_HEAD_DIM ** -0.5 * LOG2E
    for bb in range(GQA_WIDTH // LANES):
        blk = p[:, A_QG + bb * LANES:A_QG + (bb + 1) * LANES]
        blk = _rope(_head_rms_pair(blk, gqg_ref[...]), cos_g, sf_g, sb_g, GQA_HEAD_DIM // 2)
        qgo_ref[:, bb * LANES:(bb + 1) * LANES] = (blk * sg).astype(BF)

    for g in range(F_GROUPS):
        fg = p[:, A_F + g * LANES:A_F + (g + 1) * LANES].astype(BF)
        r = _dot(fg, cs_ref[...])
        uc_ref[:, g * LANES:(g + 1) * LANES] = r[:, :LANES].astype(BF)
        us_ref[:, g * LANES:(g + 1) * LANES] = r[:, LANES:].astype(BF)


def _full(shape):
    zeros = (0,) * len(shape)
    return pl.BlockSpec(shape, lambda *_: zeros)


def _proj(x_all, mods, rope, wa, wk, wuv, wuq, erep, cs128, kvg, qg, gqg, gkg):
    tm = TM_PROJ
    tok = lambda w: pl.BlockSpec((None, tm, w), lambda b, t: (b, t, 0))
    widths = (MLA_HEADS * LANES, MLA_WIDTH, GQA_WIDTH, GQA_WIDTH, MLA_HEADS * LANES, GQA_WIDTH, F_WIDTH, F_WIDTH)
    return pl.pallas_call(
        _proj_kernel,
        grid=(BATCH, T_ALL // tm),
        in_specs=[
            tok(D_MODEL),
            pl.BlockSpec((None, N_MOD, D_MODEL), lambda b, t: (b, 0, 0)),
            pl.BlockSpec((None, N_MOD, D_MODEL), lambda b, t: (BATCH, 0, 0)),
            pl.BlockSpec((6, tm, LANES), lambda b, t: (0, t, 0)),
            _full(wa.shape), _full(wk.shape), _full(wuv.shape), _full(wuq.shape),
            _full(erep.shape), _full(cs128.shape),
            _full(kvg.shape), _full(qg.shape), _full(gqg.shape), _full(gkg.shape),
        ],
        out_specs=[tok(w) for w in widths],
        out_shape=[jax.ShapeDtypeStruct((BATCH, T_ALL, w), BF) for w in widths],
        compiler_params=pltpu.CompilerParams(
            dimension_semantics=("parallel", "parallel"), vmem_limit_bytes=VMEM_LIMIT_BYTES),
        name="proj",
    )(x_all, mods, mods, rope, wa, wk, wuv, wuq, erep, cs128, kvg, qg, gqg, gkg)


def _fourier_kernel(uc_ref, us_ref, cl_ref, sl_ref, cc_ref, sc_ref, o_ref):
    rt = pl.program_id(1)
    n_lat = SEQ // TR

    @pl.when(rt < n_lat)
    def _():
        y = _dot(cl_ref[...], uc_ref[:SEQ, :]) + _dot(sl_ref[...], us_ref[:SEQ, :])
        o_ref[...] = (y * (SEQ * F_GROUP_DIM) ** -0.5).astype(BF)

    @pl.when(rt >= n_lat)
    def _():
        y = _dot(cc_ref[...], uc_ref[SEQ:, :]) + _dot(sc_ref[...], us_ref[SEQ:, :])
        o_ref[...] = (y * (CTX_LEN * F_GROUP_DIM) ** -0.5).astype(BF)


def _fourier(uc, us, cl, sl, cc, sc):
    n_lat = SEQ // TR
    whole = pl.BlockSpec((None, T_ALL, F_WIDTH), lambda b, r: (b, 0, 0))
    tab = pl.BlockSpec((TR, SEQ), lambda b, r: (jnp.minimum(r, n_lat - 1), 0))
    return pl.pallas_call(
        _fourier_kernel,
        grid=(BATCH, T_ALL // TR),
        in_specs=[whole, whole, tab, tab, _full(cc.shape), _full(sc.shape)],
        out_specs=pl.BlockSpec((None, TR, F_WIDTH), lambda b, r: (b, r, 0)),
        out_shape=jax.ShapeDtypeStruct((BATCH, T_ALL, F_WIDTH), BF),
        compiler_params=pltpu.CompilerParams(
            dimension_semantics=("parallel", "arbitrary"), vmem_limit_bytes=VMEM_LIMIT_BYTES),
        name="fourier",
    )(uc, us, cl, sl, cc, sc)


def _attn_heads(q, k_ref, v_ref, k_lo, n_keys, head_w):
    heads_per_chunk = MXU_DIM // head_w
    v = v_ref[k_lo:k_lo + n_keys, :]
    lane = lax.broadcasted_iota(jnp.int32, (q.shape[0], MXU_DIM), 1)
    acc = jnp.zeros((q.shape[0], MXU_DIM), F32)
    for j in range(HEAD_GROUP):
        ch = j // heads_per_chunk
        lo = (j % heads_per_chunk) * head_w
        qc = q[:, ch * MXU_DIM:(ch + 1) * MXU_DIM]
        qj = jnp.where((lane >= lo) & (lane < lo + head_w), qc, jnp.zeros_like(qc))
        kc = k_ref[k_lo:k_lo + n_keys, ch * MXU_DIM:(ch + 1) * MXU_DIM]
        s = lax.dot_general(qj, kc, (((1,), (1,)), ((), ())), preferred_element_type=F32)
        m = jnp.max(s, axis=-1, keepdims=True)
        e = jnp.exp2(s - m)
        inv = 1.0 / jnp.sum(e, axis=-1, keepdims=True)
        o = _dot(e.astype(BF), v)
        vlo = j * MLA_V_DIM
        acc = acc + o * jnp.where((lane >= vlo) & (lane < vlo + MLA_V_DIM), inv, 0.0)
    return acc


def _attn_kernel(q_ref, k_ref, v_ref, o_ref, *, head_w):
    qt = pl.program_id(2)
    n_lat = SEQ // TQ

    @pl.when(qt < n_lat)
    def _():
        o_ref[...] = _attn_heads(q_ref[...], k_ref, v_ref, 0, T_ALL, head_w).astype(BF)

    @pl.when(qt >= n_lat)
    def _():
        o_ref[...] = _attn_heads(q_ref[...], k_ref, v_ref, SEQ, CTX_LEN, head_w).astype(BF)


def _attention(q, k, v, head_w, name):
    qw = HEAD_GROUP * head_w
    n_groups = q.shape[-1] // qw
    return pl.pallas_call(
        functools.partial(_attn_kernel, head_w=head_w),
        grid=(BATCH, n_groups, T_ALL // TQ),
        in_specs=[
            pl.BlockSpec((None, TQ, qw), lambda b, g, t: (b, t, g)),
            pl.BlockSpec((None, T_ALL, qw), lambda b, g, t: (b, 0, g)),
            pl.BlockSpec((None, T_ALL, MXU_DIM), lambda b, g, t: (b, 0, g)),
        ],
        out_specs=pl.BlockSpec((None, TQ, MXU_DIM), lambda b, g, t: (b, t, g)),
        out_shape=jax.ShapeDtypeStruct((BATCH, T_ALL, n_groups * MXU_DIM), BF),
        compiler_params=pltpu.CompilerParams(
            dimension_semantics=("parallel", "parallel", "arbitrary"), vmem_limit_bytes=VMEM_LIMIT_BYTES),
        name=name,
    )(q, k, v)


def _mixout_kernel(x_ref, modx_ref, modc_ref, f_ref, am_ref, ag_ref, wg_ref, bg_ref,
                   wfo_ref, wmo_ref, wgo_ref, wo_ref, lng_ref, lnb_ref, o_ref):
    tm = x_ref.shape[0]
    is_ctx = _is_ctx_rows(pl.program_id(1), tm)
    shift = _mod_row(is_ctx, modx_ref, modc_ref, 0)
    scale = _mod_row(is_ctx, modx_ref, modc_ref, 1)
    gate1 = _mod_row(is_ctx, modx_ref, modc_ref, 2)
    x = x_ref[...]
    h = (_layer_norm(x) * (1.0 + scale) + shift).astype(BF)
    mix = None
    for i, (a_ref, w_ref) in enumerate(((f_ref, wfo_ref), (am_ref, wmo_ref), (ag_ref, wgo_ref))):
        z = _dot(h, wg_ref[:, i * D_MODEL:(i + 1) * D_MODEL]) + bg_ref[:, i * D_MODEL:(i + 1) * D_MODEL]
        term = _dot(a_ref[...], w_ref[...]) * (1.0 / (1.0 + jnp.exp(-z)))
        mix = term if mix is None else mix + term
    y = _dot(mix.astype(BF), wo_ref[...])
    o_ref[...] = _layer_norm(ALPHA * x + gate1 * y) * lng_ref[...] + lnb_ref[...]


def _mixout(x_all, mods, f, am, ag, wg, bg, wfo, wmo, wgo, wo, lng, lnb):
    tm = TM_MIX
    tok = lambda w: pl.BlockSpec((None, tm, w), lambda b, t: (b, t, 0))
    return pl.pallas_call(
        _mixout_kernel,
        grid=(BATCH, T_ALL // tm),
        in_specs=[
            tok(D_MODEL),
            pl.BlockSpec((None, N_MOD, D_MODEL), lambda b, t: (b, 0, 0)),
            pl.BlockSpec((None, N_MOD, D_MODEL), lambda b, t: (BATCH, 0, 0)),
            tok(F_WIDTH), tok(MLA_WIDTH), tok(GQA_WIDTH),
            _full(wg.shape), _full(bg.shape), _full(wfo.shape), _full(wmo.shape), _full(wgo.shape),
            _full(wo.shape), _full(lng.shape), _full(lnb.shape),
        ],
        out_specs=tok(D_MODEL),
        out_shape=jax.ShapeDtypeStruct((BATCH, T_ALL, D_MODEL), F32),
        compiler_params=pltpu.CompilerParams(
            dimension_semantics=("parallel", "parallel"), vmem_limit_bytes=VMEM_LIMIT_BYTES),
        name="mixout",
    )(x_all, mods, mods, f, am, ag, wg, bg, wfo, wmo, wgo, wo, lng, lnb)


def _mlp_kernel(x_ref, modx_ref, modc_ref, w1_ref, w2_ref, lng_ref, lnb_ref, o_ref):
    tm = x_ref.shape[0]
    is_ctx = _is_ctx_rows(pl.program_id(1), tm)
    shift = _mod_row(is_ctx, modx_ref, modc_ref, 3)
    scale = _mod_row(is_ctx, modx_ref, modc_ref, 4)
    gate2 = _mod_row(is_ctx, modx_ref, modc_ref, 5)
    x = x_ref[...]
    h = (_layer_norm(x) * (1.0 + scale) + shift).astype(BF)
    y = None
    for c in range(D_FF // FF_CHUNK):
        u = jnp.maximum(_dot(h, w1_ref[:, c * FF_CHUNK:(c + 1) * FF_CHUNK]), 0.0)
        t = _dot((u * u).astype(BF), w2_ref[c * FF_CHUNK:(c + 1) * FF_CHUNK, :])
        y = t if y is None else y + t
    o_ref[...] = _layer_norm(ALPHA * x + gate2 * y) * lng_ref[...] + lnb_ref[...]


def _mlp(x_all, mods, w1, w2, lng, lnb):
    tm = TM_MLP
    tok = pl.BlockSpec((None, tm, D_MODEL), lambda b, t: (b, t, 0))
    return pl.pallas_call(
        _mlp_kernel,
        grid=(BATCH, T_ALL // tm),
        in_specs=[
            tok,
            pl.BlockSpec((None, N_MOD, D_MODEL), lambda b, t: (b, 0, 0)),
            pl.BlockSpec((None, N_MOD, D_MODEL), lambda b, t: (BATCH, 0, 0)),
            _full(w1.shape), _full(w2.shape), _full(lng.shape), _full(lnb.shape),
        ],
        out_specs=tok,
        out_shape=jax.ShapeDtypeStruct((BATCH, T_ALL, D_MODEL), F32),
        compiler_params=pltpu.CompilerParams(
            dimension_semantics=("parallel", "parallel"), vmem_limit_bytes=VMEM_LIMIT_BYTES),
        name="mlp",
    )(x_all, mods, mods, w1, w2, lng, lnb)


def _pad_heads(w, n_heads, head_dim):
    lead = w.shape[:-1]
    w = w.reshape(*lead, n_heads, head_dim)
    w = jnp.pad(w, [(0, 0)] * len(lead) + [(0, 0), (0, LANES - head_dim)])
    return w.reshape(*lead, n_heads * LANES)


def kernel(x, c, ctx, c_ctx, w_ada, b_ada, w_in, b_gate, mla_q_g, mla_kv_g, w_uq, w_uk, w_uv,
           gqa_q_g, gqa_k_g, w_fo, w_mo, w_go, w_o, ln1_g, ln1_b, w1, w2, ln2_g, ln2_b):
    L = DEPTH
    cl_np, sl_np = _dft_tables(SEQ)
    cc_np, sc_np = _dft_tables(CTX_LEN)
    c128_np, s128_np = _dft_tables(F_GROUP_DIM)
    e_kr_np, e_rep_np = _placement_tables()
    cl, sl, cc_t, sc_t = (jnp.asarray(a).astype(BF) for a in (cl_np, sl_np, cc_np, sc_np))
    cs128 = jnp.asarray(np.concatenate([c128_np, -s128_np], axis=1)).astype(BF)
    e_rep = jnp.asarray(e_rep_np).astype(BF)
    rope = jnp.asarray(_rope_tables())

    zc = lambda n: jnp.zeros((L, D_MODEL, n), w_in.dtype)
    o_kr, o_kg, o_vg = MLA_KV_RANK, MLA_KV_RANK + MLA_ROPE_DIM, MLA_KV_RANK + MLA_ROPE_DIM + GQA_KV_WIDTH
    o_f = KV_COLS
    o_cq = o_f + F_WIDTH
    o_qg = o_cq + MLA_Q_RANK
    o_gate = o_qg + GQA_WIDTH
    wa = jnp.concatenate([
        w_in[:, :, :o_kr], w_in[:, :, o_kg:o_vg], w_in[:, :, o_vg:o_f], w_in[:, :, o_f:o_cq],
        w_in[:, :, o_cq:o_qg], w_in[:, :, o_qg:o_gate],
        zc(KR_LANE), w_in[:, :, o_kr:o_kg], zc(LANES - KR_LANE - MLA_ROPE_DIM)], axis=-1).astype(BF)
    wg = w_in[:, :, o_gate:].astype(BF)
    wk = jnp.concatenate([
        _pad_heads(w_uk, MLA_HEADS, MLA_NOPE_DIM),
        jnp.broadcast_to(jnp.asarray(e_kr_np), (L,) + e_kr_np.shape)], axis=1).astype(BF)
    wuq = _pad_heads(w_uq, MLA_HEADS, MLA_QK_DIM).astype(BF)
    wuv = w_uv.astype(BF)
    wfo, wmo, wgo, wo = (w.astype(BF) for w in (w_fo, w_mo, w_go, w_o))
    w1b, w2b = w1.astype(BF), w2.astype(BF)
    row = lambda a: a.reshape(L, 1, a.shape[-1])
    kvg, qg = row(mla_kv_g), row(mla_q_g)
    gqg = row(jnp.concatenate([gqa_q_g, gqa_q_g], axis=-1))
    gkg = row(jnp.concatenate([gqa_k_g, gqa_k_g], axis=-1))
    bg = row(b_gate)
    l1g, l1b, l2g, l2b = row(ln1_g), row(ln1_b), row(ln2_g), row(ln2_b)

    cc = jnp.concatenate([c, c_ctx[None, :], jnp.zeros((MOD_ROWS - BATCH - 1, D_MODEL), c.dtype)], axis=0)
    mods = _modulation(cc, w_ada, b_ada).reshape(L, MOD_ROWS, N_MOD, D_MODEL)

    x_all = jnp.concatenate([x, ctx], axis=1)
    for l in range(L):
        km, vm, kg, vg, qm, qgq, uc, us = _proj(
            x_all, mods[l], rope, wa[l], wk[l], wuv[l], wuq[l], e_rep, cs128, kvg[l], qg[l], gqg[l], gkg[l])
        f = _fourier(uc, us, cl, sl, cc_t, sc_t)
        am = _attention(qm, km, vm, MLA_HEAD_PAD, "attn_mla")
        ag = _attention(qgq, kg, vg, GQA_HEAD_DIM, "attn_gqa")
        x_all = _mixout(x_all, mods[l], f, am, ag, wg[l], bg[l], wfo[l], wmo[l], wgo[l], wo[l], l1g[l], l1b[l])
        x_all = _mlp(x_all, mods[l], w1b[l], w2b[l], l2g[l], l2b[l])
    return x_all[:, :SEQ, :]
```

```python
import functools
import math

import numpy as np
import jax
import jax.numpy as jnp
from jax import lax
from jax.experimental import pallas as pl
from jax.experimental.pallas import tpu as pltpu

D_MODEL = 1024
BATCH = 8
SEQ = 2048
DEPTH = 4
CTX_LEN = 256
GRID_W = 64
ROPE_THETA = 10000.0
EPS = 1e-6

F_GROUPS = 4
F_GROUP_DIM = 128
F_WIDTH = F_GROUPS * F_GROUP_DIM
MLA_HEADS = 8
MLA_Q_RANK = 256
MLA_KV_RANK = 256
MLA_NOPE_DIM = 64
MLA_ROPE_DIM = 32
MLA_QK_DIM = MLA_NOPE_DIM + MLA_ROPE_DIM
MLA_V_DIM = 64
MLA_WIDTH = MLA_HEADS * MLA_V_DIM
GQA_HEADS = 8
GQA_KV_HEADS = 2
GQA_GROUP = GQA_HEADS // GQA_KV_HEADS
GQA_HEAD_DIM = 64
GQA_WIDTH = GQA_HEADS * GQA_HEAD_DIM
GQA_KV_WIDTH = GQA_KV_HEADS * GQA_HEAD_DIM
N_BRANCHES = 3
D_FF = 4 * D_MODEL
N_MOD = 6
KV_COLS = MLA_KV_RANK + MLA_ROPE_DIM + 2 * GQA_KV_WIDTH

T_ALL = SEQ + CTX_LEN
ALPHA = (2.0 * DEPTH) ** 0.25
LOG2E = math.log2(math.e)

LANES = 128
MXU_DIM = 256
VMEM_LIMIT_BYTES = 56 * 1024 * 1024

MLA_HEAD_PAD = LANES
HEAD_GROUP = 4
MOD_ROWS = 16

A_CKV, A_KG, A_VG, A_F, A_CQ, A_QG, A_KR = 0, 256, 384, 512, 1024, 1280, 1792
A_COLS = 1920
KR_LANE = 64

TM_ALL = 768
TM_LAT = 512
N_SUB = 4
TQ = 512
TR = 512
FF_CHUNK = 1024

BF = jnp.bfloat16
F32 = jnp.float32


def _dft_tables(n):
    k = np.arange(n, dtype=np.int64)
    ang = 2.0 * np.pi * ((k[:, None] * k[None, :]) % n).astype(np.float64) / n
    return np.cos(ang).astype(np.float32), (-np.sin(ang)).astype(np.float32)


def _rope_tables():
    t = np.arange(SEQ)
    rows = (t // GRID_W).astype(np.float64)
    cols = (t % GRID_W).astype(np.float64)

    def angles(d_rot):
        n = d_rot // 4
        freqs = ROPE_THETA ** (-np.arange(n, dtype=np.float64) / n)
        return np.concatenate([rows[:, None] * freqs, cols[:, None] * freqs], axis=-1)

    out = np.zeros((6, T_ALL, LANES), np.float64)
    out[0] = 1.0
    out[3] = 1.0
    a = angles(MLA_ROPE_DIM)
    h = MLA_ROPE_DIM // 2
    b0 = MLA_NOPE_DIM
    out[0, :SEQ, b0:b0 + h] = np.cos(a)
    out[0, :SEQ, b0 + h:b0 + 2 * h] = np.cos(a)
    out[1, :SEQ, b0 + h:b0 + 2 * h] = np.sin(a)
    out[2, :SEQ, b0:b0 + h] = -np.sin(a)
    a = angles(GQA_HEAD_DIM)
    h = GQA_HEAD_DIM // 2
    for b0 in (0, GQA_HEAD_DIM):
        out[3, :SEQ, b0:b0 + h] = np.cos(a)
        out[3, :SEQ, b0 + h:b0 + 2 * h] = np.cos(a)
        out[4, :SEQ, b0 + h:b0 + 2 * h] = np.sin(a)
        out[5, :SEQ, b0:b0 + h] = -np.sin(a)
    return out.astype(np.float32)


def _placement_tables():
    e_kr = np.zeros((LANES, MLA_HEADS * MLA_HEAD_PAD), np.float32)
    for hh in range(MLA_HEADS):
        for d in range(MLA_ROPE_DIM):
            e_kr[KR_LANE + d, hh * MLA_HEAD_PAD + MLA_NOPE_DIM + d] = 1.0
    e_rep = np.zeros((GQA_KV_WIDTH, GQA_WIDTH), np.float32)
    for g in range(GQA_KV_HEADS):
        for j in range(GQA_GROUP):
            for d in range(GQA_HEAD_DIM):
                e_rep[g * GQA_HEAD_DIM + d, (g * GQA_GROUP + j) * GQA_HEAD_DIM + d] = 1.0
    return e_kr, e_rep


def _layer_norm(x):
    mu = jnp.mean(x, axis=-1, keepdims=True)
    xc = x - mu
    var = jnp.mean(xc * xc, axis=-1, keepdims=True)
    return xc * lax.rsqrt(var + EPS)


def _rms(x):
    return x * lax.rsqrt(jnp.mean(x * x, axis=-1, keepdims=True) + EPS)


def _sub_rows(tm):
    sub = tm // N_SUB
    return [slice(i * sub, (i + 1) * sub) for i in range(N_SUB)]


def _pipelined(subs, first, second):
    nxt = first(subs[0])
    for i, rows in enumerate(subs):
        cur = nxt
        if i + 1 < len(subs):
            nxt = first(subs[i + 1])
        second(rows, cur)


def _is_ctx_rows(tile_idx, tm, rows):
    n = rows.stop - rows.start
    row = tile_idx * tm + rows.start + lax.broadcasted_iota(jnp.int32, (n, 1), 0)
    return row >= SEQ


def _mod_row(is_ctx, modx_ref, modc_ref, i):
    return jnp.where(is_ctx, modc_ref[i:i + 1, :], modx_ref[i:i + 1, :])


def _dot(a, b):
    return jnp.dot(a, b, preferred_element_type=F32)


def _rope(x, cos, sin_fwd, sin_bwd, half):
    return x * cos + pltpu.roll(x, half, 1) * sin_fwd + pltpu.roll(x, LANES - half, 1) * sin_bwd


def _head_rms_pair(x, gain):
    lane = lax.broadcasted_iota(jnp.int32, x.shape, 1)
    first = lane < GQA_HEAD_DIM
    sq = x * x
    s0 = jnp.sum(jnp.where(first, sq, 0.0), axis=-1, keepdims=True)
    s1 = jnp.sum(jnp.where(first, 0.0, sq), axis=-1, keepdims=True)
    ms = jnp.where(first, s0, s1) * (1.0 / GQA_HEAD_DIM)
    return x * lax.rsqrt(ms + EPS) * gain


def _mod_kernel(cc_ref, w_ref, b_ref, o_ref):
    cc = cc_ref[...]
    s = cc * (1.0 / (1.0 + jnp.exp(-cc)))
    o_ref[...] = jnp.dot(s, w_ref[...], preferred_element_type=F32,
                         precision=lax.Precision.HIGHEST) + b_ref[...]


def _modulation(cc, w_ada, b_ada):
    nj = N_MOD
    return pl.pallas_call(
        _mod_kernel,
        grid=(DEPTH, nj),
        in_specs=[
            pl.BlockSpec((MOD_ROWS, D_MODEL), lambda l, j: (0, 0)),
            pl.BlockSpec((None, D_MODEL, D_MODEL), lambda l, j: (l, 0, j)),
            pl.BlockSpec((None, 1, D_MODEL), lambda l, j: (l, 0, j)),
        ],
        out_specs=pl.BlockSpec((None, MOD_ROWS, D_MODEL), lambda l, j: (l, 0, j)),
        out_shape=jax.ShapeDtypeStruct((DEPTH, MOD_ROWS, N_MOD * D_MODEL), F32),
        compiler_params=pltpu.CompilerParams(
            dimension_semantics=("arbitrary", "arbitrary"), vmem_limit_bytes=VMEM_LIMIT_BYTES),
        name="modulation",
    )(cc, w_ada, b_ada.reshape(DEPTH, 1, N_MOD * D_MODEL))


def _proj_kernel(x_ref, modx_ref, modc_ref, rope_ref, wa_ref, wk_ref, wuv_ref, wuq_ref,
                 erep_ref, cs_ref, kvg_ref, qg_ref, gqg_ref, gkg_ref,
                 km_ref, vm_ref, kg_ref, vg_ref, qm_ref, qgo_ref, uc_ref, us_ref):
    tm = x_ref.shape[0]

    def project(rows):
        is_ctx = _is_ctx_rows(pl.program_id(1), tm, rows)
        shift = _mod_row(is_ctx, modx_ref, modc_ref, 0)
        scale = _mod_row(is_ctx, modx_ref, modc_ref, 1)
        h = (_layer_norm(x_ref[rows, :]) * (1.0 + scale) + shift).astype(BF)
        return _dot(h, wa_ref[...])

    def branches(rows, p):
        cos_m, sf_m, sb_m = rope_ref[0, rows, :], rope_ref[1, rows, :], rope_ref[2, rows, :]
        cos_g, sf_g, sb_g = rope_ref[3, rows, :], rope_ref[4, rows, :], rope_ref[5, rows, :]

        ckv = (_rms(p[:, A_CKV:A_CKV + MLA_KV_RANK]) * kvg_ref[...]).astype(BF)
        kr = _rope(p[:, A_KR:A_KR + LANES], cos_m, sf_m, sb_m, MLA_ROPE_DIM // 2).astype(BF)
        km_ref[rows, :] = _dot(jnp.concatenate([ckv, kr], axis=-1), wk_ref[...]).astype(BF)
        vm_ref[rows, :] = _dot(ckv, wuv_ref[...]).astype(BF)

        kg = _rope(_head_rms_pair(p[:, A_KG:A_KG + LANES], gkg_ref[...]), cos_g, sf_g, sb_g, GQA_HEAD_DIM // 2)
        kg_ref[rows, :] = _dot(kg.astype(BF), erep_ref[...]).astype(BF)
        vg_ref[rows, :] = _dot(p[:, A_VG:A_VG + LANES].astype(BF), erep_ref[...]).astype(BF)

        cq = (_rms(p[:, A_CQ:A_CQ + MLA_Q_RANK]) * qg_ref[...]).astype(BF)
        qm = _dot(cq, wuq_ref[...])
        sm = MLA_QK_DIM ** -0.5 * LOG2E
        for hh in range(MLA_HEADS):
            blk = qm[:, hh * LANES:(hh + 1) * LANES]
            qm_ref[rows, hh * LANES:(hh + 1) * LANES] = (
                _rope(blk, cos_m, sf_m, sb_m, MLA_ROPE_DIM // 2) * sm).astype(BF)

        sg = GQA_HEAD_DIM ** -0.5 * LOG2E
        for bb in range(GQA_WIDTH // LANES):
            blk = p[:, A_QG + bb * LANES:A_QG + (bb + 1) * LANES]
            blk = _rope(_head_rms_pair(blk, gqg_ref[...]), cos_g, sf_g, sb_g, GQA_HEAD_DIM // 2)
            qgo_ref[rows, bb * LANES:(bb + 1) * LANES] = (blk * sg).astype(BF)

        for g in range(F_GROUPS):
            fg = p[:, A_F + g * LANES:A_F + (g + 1) * LANES].astype(BF)
            r = _dot(fg, cs_ref[...])
            uc_ref[rows, g * LANES:(g + 1) * LANES] = r[:, :LANES].astype(BF)
            us_ref[rows, g * LANES:(g + 1) * LANES] = r[:, LANES:].astype(BF)

    _pipelined(_sub_rows(tm), project, branches)


def _full(shape):
    zeros = (0,) * len(shape)
    return pl.BlockSpec(shape, lambda *_: zeros, pipeline_mode=pl.Buffered(1))


def _proj(x_all, mods, rope, wa, wk, wuv, wuq, erep, cs128, kvg, qg, gqg, gkg):
    tm = TM_ALL
    tok = lambda w: pl.BlockSpec((None, tm, w), lambda b, t: (b, t, 0))
    widths = (MLA_HEADS * LANES, MLA_WIDTH, GQA_WIDTH, GQA_WIDTH, MLA_HEADS * LANES, GQA_WIDTH, F_WIDTH, F_WIDTH)
    return pl.pallas_call(
        _proj_kernel,
        grid=(BATCH, T_ALL // tm),
        in_specs=[
            tok(D_MODEL),
            pl.BlockSpec((None, N_MOD, D_MODEL), lambda b, t: (b, 0, 0)),
            pl.BlockSpec((None, N_MOD, D_MODEL), lambda b, t: (BATCH, 0, 0)),
            pl.BlockSpec((6, tm, LANES), lambda b, t: (0, t, 0)),
            _full(wa.shape), _full(wk.shape), _full(wuv.shape), _full(wuq.shape),
            _full(erep.shape), _full(cs128.shape),
            _full(kvg.shape), _full(qg.shape), _full(gqg.shape), _full(gkg.shape),
        ],
        out_specs=[tok(w) for w in widths],
        out_shape=[jax.ShapeDtypeStruct((BATCH, T_ALL, w), BF) for w in widths],
        compiler_params=pltpu.CompilerParams(
            dimension_semantics=("parallel", "parallel"), vmem_limit_bytes=VMEM_LIMIT_BYTES),
        name="proj",
    )(x_all, mods, mods, rope, wa, wk, wuv, wuq, erep, cs128, kvg, qg, gqg, gkg)


def _fourier_kernel(uc_ref, us_ref, cl_ref, sl_ref, cc_ref, sc_ref, o_ref, *, with_ctx):
    rt = pl.program_id(1)
    n_lat = SEQ // TR

    def latent():
        y = _dot(cl_ref[...], uc_ref[:SEQ, :]) + _dot(sl_ref[...], us_ref[:SEQ, :])
        o_ref[...] = (y * (SEQ * F_GROUP_DIM) ** -0.5).astype(BF)

    if not with_ctx:
        latent()
        return
    pl.when(rt < n_lat)(latent)

    @pl.when(rt >= n_lat)
    def _():
        y = _dot(cc_ref[...], uc_ref[SEQ:, :]) + _dot(sc_ref[...], us_ref[SEQ:, :])
        o_ref[:CTX_LEN, :] = (y * (CTX_LEN * F_GROUP_DIM) ** -0.5).astype(BF)


def _fourier(uc, us, cl, sl, cc, sc, with_ctx):
    n_lat = SEQ // TR
    n_rows = T_ALL if with_ctx else SEQ
    whole = pl.BlockSpec((None, T_ALL, F_WIDTH), lambda b, r: (b, 0, 0))
    tab = pl.BlockSpec((TR, SEQ), lambda b, r: (jnp.minimum(r, n_lat - 1), 0))
    return pl.pallas_call(
        functools.partial(_fourier_kernel, with_ctx=with_ctx),
        grid=(BATCH, pl.cdiv(n_rows, TR)),
        in_specs=[whole, whole, tab, tab, _full(cc.shape), _full(sc.shape)],
        out_specs=pl.BlockSpec((None, TR, F_WIDTH), lambda b, r: (b, r, 0)),
        out_shape=jax.ShapeDtypeStruct((BATCH, n_rows, F_WIDTH), BF),
        compiler_params=pltpu.CompilerParams(
            dimension_semantics=("parallel", "arbitrary"), vmem_limit_bytes=VMEM_LIMIT_BYTES),
        name="fourier",
    )(uc, us, cl, sl, cc, sc)


N_HEADS = 8
ROW_UNIT = 256


def _attn_rows(q, k_ref, v_ref, k_lo, n_keys, head_w):
    heads_per_chunk = MXU_DIM // head_w
    lane = lax.broadcasted_iota(jnp.int32, (q.shape[0], MXU_DIM), 1)
    outs = []
    for vg in range(N_HEADS // HEAD_GROUP):
        v = v_ref[k_lo:k_lo + n_keys, vg * MXU_DIM:(vg + 1) * MXU_DIM]
        acc = None
        for jj in range(HEAD_GROUP):
            j = vg * HEAD_GROUP + jj
            ch = j // heads_per_chunk
            lo = (j % heads_per_chunk) * head_w
            qc = q[:, ch * MXU_DIM:(ch + 1) * MXU_DIM]
            qj = jnp.where((lane >= lo) & (lane < lo + head_w), qc, jnp.zeros_like(qc))
            kc = k_ref[k_lo:k_lo + n_keys, ch * MXU_DIM:(ch + 1) * MXU_DIM]
            s = lax.dot_general(qj, kc, (((1,), (1,)), ((), ())), preferred_element_type=F32)
            m = jnp.max(s, axis=-1, keepdims=True)
            e = jnp.exp2(s - m)
            inv = 1.0 / jnp.sum(e, axis=-1, keepdims=True)
            o = _dot(e.astype(BF), v)
            vlo = jj * MLA_V_DIM
            term = o * jnp.where((lane >= vlo) & (lane < vlo + MLA_V_DIM), inv, 0.0)
            acc = term if acc is None else acc + term
        outs.append(acc)
    return jnp.concatenate(outs, axis=-1)


def _attn_kernel(q_ref, k_ref, v_ref, o_ref, *, head_w, with_ctx):
    qt = pl.program_id(1)
    n_lat = SEQ // TQ

    def latent():
        for r in range(TQ // ROW_UNIT):
            rows = slice(r * ROW_UNIT, (r + 1) * ROW_UNIT)
            o_ref[rows, :] = _attn_rows(q_ref[rows, :], k_ref, v_ref, 0, T_ALL, head_w).astype(BF)

    if not with_ctx:
        latent()
        return
    pl.when(qt < n_lat)(latent)

    @pl.when(qt >= n_lat)
    def _():
        o_ref[:CTX_LEN, :] = _attn_rows(q_ref[:CTX_LEN, :], k_ref, v_ref, SEQ, CTX_LEN, head_w).astype(BF)


def _attention(q, k, v, head_w, with_ctx, name):
    qw = N_HEADS * head_w
    vw = N_HEADS * MLA_V_DIM
    n_rows = T_ALL if with_ctx else SEQ
    return pl.pallas_call(
        functools.partial(_attn_kernel, head_w=head_w, with_ctx=with_ctx),
        grid=(BATCH, pl.cdiv(n_rows, TQ)),
        in_specs=[
            pl.BlockSpec((None, TQ, qw), lambda b, t: (b, t, 0)),
            pl.BlockSpec((None, T_ALL, qw), lambda b, t: (b, 0, 0)),
            pl.BlockSpec((None, T_ALL, vw), lambda b, t: (b, 0, 0)),
        ],
        out_specs=pl.BlockSpec((None, TQ, vw), lambda b, t: (b, t, 0)),
        out_shape=jax.ShapeDtypeStruct((BATCH, n_rows, vw), BF),
        compiler_params=pltpu.CompilerParams(
            dimension_semantics=("parallel", "arbitrary"), vmem_limit_bytes=VMEM_LIMIT_BYTES),
        name=name,
    )(q, k, v)


def _mixout_kernel(x_ref, modx_ref, modc_ref, f_ref, am_ref, ag_ref, wg_ref, bg_ref,
                   wfo_ref, wmo_ref, wgo_ref, wo_ref, lng_ref, lnb_ref, o_ref):
    tm = x_ref.shape[0]

    def modulated(rows):
        is_ctx = _is_ctx_rows(pl.program_id(1), tm, rows)
        shift = _mod_row(is_ctx, modx_ref, modc_ref, 0)
        scale = _mod_row(is_ctx, modx_ref, modc_ref, 1)
        return (_layer_norm(x_ref[rows, :]) * (1.0 + scale) + shift).astype(BF)

    def mix_and_norm(rows, h):
        mix = None
        for i, (a_ref, w_ref) in enumerate(((f_ref, wfo_ref), (am_ref, wmo_ref), (ag_ref, wgo_ref))):
            z = _dot(h, wg_ref[:, i * D_MODEL:(i + 1) * D_MODEL]) + bg_ref[:, i * D_MODEL:(i + 1) * D_MODEL]
            term = _dot(a_ref[rows, :], w_ref[...]) * (1.0 / (1.0 + jnp.exp(-z)))
            mix = term if mix is None else mix + term
        y = _dot(mix.astype(BF), wo_ref[...])
        gate1 = _mod_row(_is_ctx_rows(pl.program_id(1), tm, rows), modx_ref, modc_ref, 2)
        o_ref[rows, :] = _layer_norm(ALPHA * x_ref[rows, :] + gate1 * y) * lng_ref[...] + lnb_ref[...]

    _pipelined(_sub_rows(tm), modulated, mix_and_norm)


def _token_grid(with_ctx):
    return (TM_ALL, T_ALL) if with_ctx else (TM_LAT, SEQ)


def _mixout(x_all, mods, f, am, ag, wg, bg, wfo, wmo, wgo, wo, lng, lnb, with_ctx):
    tm, n_rows = _token_grid(with_ctx)
    tok = lambda w: pl.BlockSpec((None, tm, w), lambda b, t: (b, t, 0))
    return pl.pallas_call(
        _mixout_kernel,
        grid=(BATCH, n_rows // tm),
        in_specs=[
            tok(D_MODEL),
            pl.BlockSpec((None, N_MOD, D_MODEL), lambda b, t: (b, 0, 0)),
            pl.BlockSpec((None, N_MOD, D_MODEL), lambda b, t: (BATCH, 0, 0)),
            tok(F_WIDTH), tok(MLA_WIDTH), tok(GQA_WIDTH),
            _full(wg.shape), _full(bg.shape), _full(wfo.shape), _full(wmo.shape), _full(wgo.shape),
            _full(wo.shape), _full(lng.shape), _full(lnb.shape),
        ],
        out_specs=tok(D_MODEL),
        out_shape=jax.ShapeDtypeStruct((BATCH, n_rows, D_MODEL), F32),
        compiler_params=pltpu.CompilerParams(
            dimension_semantics=("parallel", "parallel"), vmem_limit_bytes=VMEM_LIMIT_BYTES),
        name="mixout",
    )(x_all, mods, mods, f, am, ag, wg, bg, wfo, wmo, wgo, wo, lng, lnb)


def _mlp_kernel(x_ref, modx_ref, modc_ref, w1_ref, w2_ref, lng_ref, lnb_ref, o_ref):
    tm = x_ref.shape[0]

    def modulated(rows):
        is_ctx = _is_ctx_rows(pl.program_id(1), tm, rows)
        shift = _mod_row(is_ctx, modx_ref, modc_ref, 3)
        scale = _mod_row(is_ctx, modx_ref, modc_ref, 4)
        return (_layer_norm(x_ref[rows, :]) * (1.0 + scale) + shift).astype(BF)

    def mlp_and_norm(rows, h):
        y = None
        for c in range(D_FF // FF_CHUNK):
            u = jnp.maximum(_dot(h, w1_ref[:, c * FF_CHUNK:(c + 1) * FF_CHUNK]), 0.0)
            t = _dot((u * u).astype(BF), w2_ref[c * FF_CHUNK:(c + 1) * FF_CHUNK, :])
            y = t if y is None else y + t
        gate2 = _mod_row(_is_ctx_rows(pl.program_id(1), tm, rows), modx_ref, modc_ref, 5)
        o_ref[rows, :] = _layer_norm(ALPHA * x_ref[rows, :] + gate2 * y) * lng_ref[...] + lnb_ref[...]

    _pipelined(_sub_rows(tm), modulated, mlp_and_norm)


def _mlp(x_all, mods, w1, w2, lng, lnb, with_ctx):
    tm, n_rows = _token_grid(with_ctx)
    tok = pl.BlockSpec((None, tm, D_MODEL), lambda b, t: (b, t, 0))
    return pl.pallas_call(
        _mlp_kernel,
        grid=(BATCH, n_rows // tm),
        in_specs=[
            tok,
            pl.BlockSpec((None, N_MOD, D_MODEL), lambda b, t: (b, 0, 0)),
            pl.BlockSpec((None, N_MOD, D_MODEL), lambda b, t: (BATCH, 0, 0)),
            _full(w1.shape), _full(w2.shape), _full(lng.shape), _full(lnb.shape),
        ],
        out_specs=tok,
        out_shape=jax.ShapeDtypeStruct((BATCH, n_rows, D_MODEL), F32),
        compiler_params=pltpu.CompilerParams(
            dimension_semantics=("parallel", "parallel"), vmem_limit_bytes=VMEM_LIMIT_BYTES),
        name="mlp",
    )(x_all, mods, mods, w1, w2, lng, lnb)


def _pad_heads(w, n_heads, head_dim):
    lead = w.shape[:-1]
    w = w.reshape(*lead, n_heads, head_dim)
    w = jnp.pad(w, [(0, 0)] * len(lead) + [(0, 0), (0, LANES - head_dim)])
    return w.reshape(*lead, n_heads * LANES)


def kernel(x, c, ctx, c_ctx, w_ada, b_ada, w_in, b_gate, mla_q_g, mla_kv_g, w_uq, w_uk, w_uv,
           gqa_q_g, gqa_k_g, w_fo, w_mo, w_go, w_o, ln1_g, ln1_b, w1, w2, ln2_g, ln2_b):
    L = DEPTH
    cl_np, sl_np = _dft_tables(SEQ)
    cc_np, sc_np = _dft_tables(CTX_LEN)
    c128_np, s128_np = _dft_tables(F_GROUP_DIM)
    e_kr_np, e_rep_np = _placement_tables()
    cl, sl, cc_t, sc_t = (jnp.asarray(a).astype(BF) for a in (cl_np, sl_np, cc_np, sc_np))
    cs128 = jnp.asarray(np.concatenate([c128_np, -s128_np], axis=1)).astype(BF)
    e_rep = jnp.asarray(e_rep_np).astype(BF)
    rope = jnp.asarray(_rope_tables())

    zc = lambda n: jnp.zeros((L, D_MODEL, n), w_in.dtype)
    o_kr, o_kg, o_vg = MLA_KV_RANK, MLA_KV_RANK + MLA_ROPE_DIM, MLA_KV_RANK + MLA_ROPE_DIM + GQA_KV_WIDTH
    o_f = KV_COLS
    o_cq = o_f + F_WIDTH
    o_qg = o_cq + MLA_Q_RANK
    o_gate = o_qg + GQA_WIDTH
    wa = jnp.concatenate([
        w_in[:, :, :o_kr], w_in[:, :, o_kg:o_vg], w_in[:, :, o_vg:o_f], w_in[:, :, o_f:o_cq],
        w_in[:, :, o_cq:o_qg], w_in[:, :, o_qg:o_gate],
        zc(KR_LANE), w_in[:, :, o_kr:o_kg], zc(LANES - KR_LANE - MLA_ROPE_DIM)], axis=-1).astype(BF)
    wg = w_in[:, :, o_gate:].astype(BF)
    wk = jnp.concatenate([
        _pad_heads(w_uk, MLA_HEADS, MLA_NOPE_DIM),
        jnp.broadcast_to(jnp.asarray(e_kr_np), (L,) + e_kr_np.shape)], axis=1).astype(BF)
    wuq = _pad_heads(w_uq, MLA_HEADS, MLA_QK_DIM).astype(BF)
    wuv = w_uv.astype(BF)
    wfo, wmo, wgo, wo = (w.astype(BF) for w in (w_fo, w_mo, w_go, w_o))
    w1b, w2b = w1.astype(BF), w2.astype(BF)
    row = lambda a: a.reshape(L, 1, a.shape[-1])
    kvg, qg = row(mla_kv_g), row(mla_q_g)
    gqg = row(jnp.concatenate([gqa_q_g, gqa_q_g], axis=-1))
    gkg = row(jnp.concatenate([gqa_k_g, gqa_k_g], axis=-1))
    bg = row(b_gate)
    l1g, l1b, l2g, l2b = row(ln1_g), row(ln1_b), row(ln2_g), row(ln2_b)

    cc = jnp.concatenate([c, c_ctx[None, :], jnp.zeros((MOD_ROWS - BATCH - 1, D_MODEL), c.dtype)], axis=0)
    mods = _modulation(cc, w_ada, b_ada).reshape(L, MOD_ROWS, N_MOD, D_MODEL)

    x_all = jnp.concatenate([x, ctx], axis=1)
    for l in range(L):
        km, vm, kg, vg, qm, qgq, uc, us = _proj(
            x_all, mods[l], rope, wa[l], wk[l], wuv[l], wuq[l], e_rep, cs128, kvg[l], qg[l], gqg[l], gkg[l])
        with_ctx = l < L - 1
        f = _fourier(uc, us, cl, sl, cc_t, sc_t, with_ctx)
        am = _attention(qm, km, vm, MLA_HEAD_PAD, with_ctx, "attn_mla")
        ag = _attention(qgq, kg, vg, GQA_HEAD_DIM, with_ctx, "attn_gqa")
        x_all = _mixout(x_all, mods[l], f, am, ag, wg[l], bg[l], wfo[l], wmo[l], wgo[l], wo[l], l1g[l], l1b[l],
                        with_ctx)
        x_all = _mlp(x_all, mods[l], w1b[l], w2b[l], l2g[l], l2b[l], with_ctx)
    return x_all
```

```python
import functools
import math

import numpy as np
import jax
import jax.numpy as jnp
from jax import lax
from jax.experimental import pallas as pl
from jax.experimental.pallas import tpu as pltpu

D_MODEL = 1024
BATCH = 8
SEQ = 2048
DEPTH = 4
CTX_LEN = 256
GRID_W = 64
ROPE_THETA = 10000.0
EPS = 1e-6

F_GROUPS = 4
F_GROUP_DIM = 128
F_WIDTH = F_GROUPS * F_GROUP_DIM
MLA_HEADS = 8
MLA_Q_RANK = 256
MLA_KV_RANK = 256
MLA_NOPE_DIM = 64
MLA_ROPE_DIM = 32
MLA_QK_DIM = MLA_NOPE_DIM + MLA_ROPE_DIM
MLA_V_DIM = 64
MLA_WIDTH = MLA_HEADS * MLA_V_DIM
GQA_HEADS = 8
GQA_KV_HEADS = 2
GQA_GROUP = GQA_HEADS // GQA_KV_HEADS
GQA_HEAD_DIM = 64
GQA_WIDTH = GQA_HEADS * GQA_HEAD_DIM
GQA_KV_WIDTH = GQA_KV_HEADS * GQA_HEAD_DIM
N_BRANCHES = 3
D_FF = 4 * D_MODEL
N_MOD = 6
KV_COLS = MLA_KV_RANK + MLA_ROPE_DIM + 2 * GQA_KV_WIDTH

T_ALL = SEQ + CTX_LEN
ALPHA = (2.0 * DEPTH) ** 0.25
LOG2E = math.log2(math.e)

LANES = 128
MXU_DIM = 256
VMEM_LIMIT_BYTES = 56 * 1024 * 1024

MLA_HEAD_PAD = LANES
HEAD_GROUP = 4
MOD_ROWS = 16

A_CKV, A_KG, A_VG, A_F, A_CQ, A_QG, A_KR = 0, 256, 384, 512, 1024, 1280, 1792
A_COLS = 1920
KR_LANE = 64

TM_ALL = 768
TM_LAT = 512
N_SUB = 4
PROJ_SUB = 256
ONES_LANE = 64
TQ = 512
TR = 512
FF_CHUNK = 1024

BF = jnp.bfloat16
F32 = jnp.float32


def _dft_tables(n):
    k = np.arange(n, dtype=np.int64)
    ang = 2.0 * np.pi * ((k[:, None] * k[None, :]) % n).astype(np.float64) / n
    return np.cos(ang).astype(np.float32), (-np.sin(ang)).astype(np.float32)


def _rope_tables():
    t = np.arange(SEQ)
    rows = (t // GRID_W).astype(np.float64)
    cols = (t % GRID_W).astype(np.float64)

    def angles(d_rot):
        n = d_rot // 4
        freqs = ROPE_THETA ** (-np.arange(n, dtype=np.float64) / n)
        return np.concatenate([rows[:, None] * freqs, cols[:, None] * freqs], axis=-1)

    out = np.zeros((6, T_ALL, LANES), np.float64)
    out[0] = 1.0
    out[3] = 1.0
    a = angles(MLA_ROPE_DIM)
    h = MLA_ROPE_DIM // 2
    b0 = MLA_NOPE_DIM
    out[0, :SEQ, b0:b0 + h] = np.cos(a)
    out[0, :SEQ, b0 + h:b0 + 2 * h] = np.cos(a)
    out[1, :SEQ, b0 + h:b0 + 2 * h] = np.sin(a)
    out[2, :SEQ, b0:b0 + h] = -np.sin(a)
    a = angles(GQA_HEAD_DIM)
    h = GQA_HEAD_DIM // 2
    for b0 in (0, GQA_HEAD_DIM):
        out[3, :SEQ, b0:b0 + h] = np.cos(a)
        out[3, :SEQ, b0 + h:b0 + 2 * h] = np.cos(a)
        out[4, :SEQ, b0 + h:b0 + 2 * h] = np.sin(a)
        out[5, :SEQ, b0:b0 + h] = -np.sin(a)
    return out.astype(np.float32)


def _placement_tables():
    e_kr = np.zeros((LANES, MLA_HEADS * MLA_HEAD_PAD), np.float32)
    for hh in range(MLA_HEADS):
        for d in range(MLA_ROPE_DIM):
            e_kr[KR_LANE + d, hh * MLA_HEAD_PAD + MLA_NOPE_DIM + d] = 1.0
    e_rep = np.zeros((GQA_KV_WIDTH, GQA_WIDTH), np.float32)
    for g in range(GQA_KV_HEADS):
        for j in range(GQA_GROUP):
            for d in range(GQA_HEAD_DIM):
                e_rep[g * GQA_HEAD_DIM + d, (g * GQA_GROUP + j) * GQA_HEAD_DIM + d] = 1.0
    e_val = np.zeros((GQA_KV_WIDTH, GQA_KV_HEADS * MXU_DIM), np.float32)
    for g in range(GQA_KV_HEADS):
        for half in range(MXU_DIM // LANES):
            for d in range(GQA_HEAD_DIM):
                e_val[g * GQA_HEAD_DIM + d, g * MXU_DIM + half * LANES + d] = 1.0
    return e_kr, e_rep, e_val


def _ones_lane(width):
    v = np.zeros((1, width), np.float32)
    v[0, ONES_LANE::LANES] = 1.0
    return v


def _layer_norm(x):
    mu = jnp.mean(x, axis=-1, keepdims=True)
    xc = x - mu
    var = jnp.mean(xc * xc, axis=-1, keepdims=True)
    return xc * lax.rsqrt(var + EPS)


def _rms(x):
    return x * lax.rsqrt(jnp.mean(x * x, axis=-1, keepdims=True) + EPS)


def _sub_rows(tm, n_sub=N_SUB):
    sub = tm // n_sub
    return [slice(i * sub, (i + 1) * sub) for i in range(n_sub)]


def _pipelined(subs, first, second):
    nxt = first(subs[0])
    for i, rows in enumerate(subs):
        cur = nxt
        if i + 1 < len(subs):
            nxt = first(subs[i + 1])
        second(rows, cur)


def _is_ctx_rows(tile_idx, tm, rows):
    n = rows.stop - rows.start
    row = tile_idx * tm + rows.start + lax.broadcasted_iota(jnp.int32, (n, 1), 0)
    return row >= SEQ


def _mod_row(is_ctx, modx_ref, modc_ref, i):
    return jnp.where(is_ctx, modc_ref[i:i + 1, :], modx_ref[i:i + 1, :])


def _dot(a, b):
    return jnp.dot(a, b, preferred_element_type=F32)


def _rope(x, cos, sin_fwd, sin_bwd, half):
    return x * cos + pltpu.roll(x, half, 1) * sin_fwd + pltpu.roll(x, LANES - half, 1) * sin_bwd


def _head_rms_pair(x, gain):
    lane = lax.broadcasted_iota(jnp.int32, x.shape, 1)
    first = lane < GQA_HEAD_DIM
    sq = x * x
    s0 = jnp.sum(jnp.where(first, sq, 0.0), axis=-1, keepdims=True)
    s1 = jnp.sum(jnp.where(first, 0.0, sq), axis=-1, keepdims=True)
    ms = jnp.where(first, s0, s1) * (1.0 / GQA_HEAD_DIM)
    return x * lax.rsqrt(ms + EPS) * gain


def _mod_kernel(cc_ref, w_ref, b_ref, o_ref):
    cc = cc_ref[...]
    s = cc * (1.0 / (1.0 + jnp.exp(-cc)))
    o_ref[...] = jnp.dot(s, w_ref[...], preferred_element_type=F32,
                         precision=lax.Precision.HIGHEST) + b_ref[...]


def _modulation(cc, w_ada, b_ada):
    nj = N_MOD
    return pl.pallas_call(
        _mod_kernel,
        grid=(DEPTH, nj),
        in_specs=[
            pl.BlockSpec((MOD_ROWS, D_MODEL), lambda l, j: (0, 0)),
            pl.BlockSpec((None, D_MODEL, D_MODEL), lambda l, j: (l, 0, j)),
            pl.BlockSpec((None, 1, D_MODEL), lambda l, j: (l, 0, j)),
        ],
        out_specs=pl.BlockSpec((None, MOD_ROWS, D_MODEL), lambda l, j: (l, 0, j)),
        out_shape=jax.ShapeDtypeStruct((DEPTH, MOD_ROWS, N_MOD * D_MODEL), F32),
        compiler_params=pltpu.CompilerParams(
            dimension_semantics=("arbitrary", "arbitrary"), vmem_limit_bytes=VMEM_LIMIT_BYTES),
        name="modulation",
    )(cc, w_ada, b_ada.reshape(DEPTH, 1, N_MOD * D_MODEL))


def _dot_nt(a, b):
    return lax.dot_general(a, b, (((1,), (1,)), ((), ())), preferred_element_type=F32)


def _proj_kernel(x_ref, modx_ref, modc_ref, rope_ref, wa_ref, wkt_ref, wuv_ref, wuq_ref,
                 erept_ref, eval_ref, cs_ref, kvg_ref, qg_ref, gqg_ref, gkg_ref, onem_ref, oneg_ref,
                 kmt_ref, vm_ref, kgt_ref, vg_ref, qm_ref, qgo_ref, uc_ref, us_ref):
    tm = x_ref.shape[0]

    def project(rows):
        is_ctx = _is_ctx_rows(pl.program_id(1), tm, rows)
        shift = _mod_row(is_ctx, modx_ref, modc_ref, 0)
        scale = _mod_row(is_ctx, modx_ref, modc_ref, 1)
        h = (_layer_norm(x_ref[rows, :]) * (1.0 + scale) + shift).astype(BF)
        return _dot(h, wa_ref[...])

    def branches(rows, p):
        cos_m, sf_m, sb_m = rope_ref[0, rows, :], rope_ref[1, rows, :], rope_ref[2, rows, :]
        cos_g, sf_g, sb_g = rope_ref[3, rows, :], rope_ref[4, rows, :], rope_ref[5, rows, :]

        ckv = (_rms(p[:, A_CKV:A_CKV + MLA_KV_RANK]) * kvg_ref[...]).astype(BF)
        kr = _rope(p[:, A_KR:A_KR + LANES], cos_m, sf_m, sb_m, MLA_ROPE_DIM // 2).astype(BF)
        kmt_ref[:, rows] = _dot_nt(wkt_ref[...], jnp.concatenate([ckv, kr], axis=-1)).astype(BF)
        vm_ref[rows, :] = (_dot(ckv, wuv_ref[...]) + onem_ref[...]).astype(BF)

        kg = _rope(_head_rms_pair(p[:, A_KG:A_KG + LANES], gkg_ref[...]), cos_g, sf_g, sb_g, GQA_HEAD_DIM // 2)
        kgt_ref[:, rows] = _dot_nt(erept_ref[...], kg.astype(BF)).astype(BF)
        vg_ref[rows, :] = (_dot(p[:, A_VG:A_VG + LANES].astype(BF), eval_ref[...]) + oneg_ref[...]).astype(BF)

        cq = (_rms(p[:, A_CQ:A_CQ + MLA_Q_RANK]) * qg_ref[...]).astype(BF)
        qm = _dot(cq, wuq_ref[...])
        sm = MLA_QK_DIM ** -0.5 * LOG2E
        for hh in range(MLA_HEADS):
            blk = qm[:, hh * LANES:(hh + 1) * LANES]
            qm_ref[rows, hh * LANES:(hh + 1) * LANES] = (
                _rope(blk, cos_m, sf_m, sb_m, MLA_ROPE_DIM // 2) * sm).astype(BF)

        sg = GQA_HEAD_DIM ** -0.5 * LOG2E
        for bb in range(GQA_WIDTH // LANES):
            blk = p[:, A_QG + bb * LANES:A_QG + (bb + 1) * LANES]
            blk = _rope(_head_rms_pair(blk, gqg_ref[...]), cos_g, sf_g, sb_g, GQA_HEAD_DIM // 2)
            qgo_ref[rows, bb * LANES:(bb + 1) * LANES] = (blk * sg).astype(BF)

        for g in range(F_GROUPS):
            fg = p[:, A_F + g * LANES:A_F + (g + 1) * LANES].astype(BF)
            r = _dot(fg, cs_ref[...])
            uc_ref[rows, g * LANES:(g + 1) * LANES] = r[:, :LANES].astype(BF)
            us_ref[rows, g * LANES:(g + 1) * LANES] = r[:, LANES:].astype(BF)

    _pipelined(_sub_rows(tm, tm // PROJ_SUB), project, branches)


def _full(shape):
    zeros = (0,) * len(shape)
    return pl.BlockSpec(shape, lambda *_: zeros, pipeline_mode=pl.Buffered(1))


def _proj(x_all, mods, rope, wa, wkt, wuv, wuq, erept, e_val, cs128, kvg, qg, gqg, gkg, onem, oneg):
    tm = TM_ALL
    tok = lambda w: pl.BlockSpec((None, tm, w), lambda b, t: (b, t, 0))
    tok_t = lambda w: pl.BlockSpec((None, w, tm), lambda b, t: (b, 0, t))
    outs = ((MLA_HEADS * LANES, True), (MLA_HEADS * LANES, False), (GQA_WIDTH, True),
            (GQA_KV_HEADS * MXU_DIM, False), (MLA_HEADS * LANES, False), (GQA_WIDTH, False),
            (F_WIDTH, False), (F_WIDTH, False))
    return pl.pallas_call(
        _proj_kernel,
        grid=(BATCH, T_ALL // tm),
        in_specs=[
            tok(D_MODEL),
            pl.BlockSpec((None, N_MOD, D_MODEL), lambda b, t: (b, 0, 0)),
            pl.BlockSpec((None, N_MOD, D_MODEL), lambda b, t: (BATCH, 0, 0)),
            pl.BlockSpec((6, tm, LANES), lambda b, t: (0, t, 0)),
            _full(wa.shape), _full(wkt.shape), _full(wuv.shape), _full(wuq.shape),
            _full(erept.shape), _full(e_val.shape), _full(cs128.shape),
            _full(kvg.shape), _full(qg.shape), _full(gqg.shape), _full(gkg.shape),
            _full(onem.shape), _full(oneg.shape),
        ],
        out_specs=[tok_t(w) if tr else tok(w) for w, tr in outs],
        out_shape=[jax.ShapeDtypeStruct((BATCH, w, T_ALL) if tr else (BATCH, T_ALL, w), BF) for w, tr in outs],
        compiler_params=pltpu.CompilerParams(
            dimension_semantics=("parallel", "parallel"), vmem_limit_bytes=VMEM_LIMIT_BYTES),
        name="proj",
    )(x_all, mods, mods, rope, wa, wkt, wuv, wuq, erept, e_val, cs128, kvg, qg, gqg, gkg, onem, oneg)


def _ctx_first(n_lat, with_ctx):
    if not with_ctx:
        return lambda t: t
    return lambda t: jnp.where(t == 0, n_lat, t - 1)


def _fourier_kernel(uc_ref, us_ref, cl_ref, sl_ref, cc_ref, sc_ref, o_ref, *, with_ctx):
    def latent():
        y = _dot(cl_ref[...], uc_ref[:SEQ, :]) + _dot(sl_ref[...], us_ref[:SEQ, :])
        o_ref[...] = (y * (SEQ * F_GROUP_DIM) ** -0.5).astype(BF)

    if not with_ctx:
        latent()
        return
    pl.when(pl.program_id(1) > 0)(latent)

    @pl.when(pl.program_id(1) == 0)
    def _():
        y = _dot(cc_ref[...], uc_ref[SEQ:, :]) + _dot(sc_ref[...], us_ref[SEQ:, :])
        o_ref[:CTX_LEN, :] = (y * (CTX_LEN * F_GROUP_DIM) ** -0.5).astype(BF)


def _fourier(uc, us, cl, sl, cc, sc, with_ctx):
    n_lat = SEQ // TR
    n_rows = T_ALL if with_ctx else SEQ
    blk = _ctx_first(n_lat, with_ctx)
    whole = pl.BlockSpec((None, T_ALL, F_WIDTH), lambda b, r: (b, 0, 0))
    tab = pl.BlockSpec((TR, SEQ), lambda b, r: (jnp.maximum(r - 1, 0) if with_ctx else r, 0))
    return pl.pallas_call(
        functools.partial(_fourier_kernel, with_ctx=with_ctx),
        grid=(BATCH, pl.cdiv(n_rows, TR)),
        in_specs=[whole, whole, tab, tab, _full(cc.shape), _full(sc.shape)],
        out_specs=pl.BlockSpec((None, TR, F_WIDTH), lambda b, r: (b, blk(r), 0)),
        out_shape=jax.ShapeDtypeStruct((BATCH, n_rows, F_WIDTH), BF),
        compiler_params=pltpu.CompilerParams(
            dimension_semantics=("parallel", "arbitrary"), vmem_limit_bytes=VMEM_LIMIT_BYTES),
        name="fourier",
    )(uc, us, cl, sl, cc, sc)


N_HEADS = 8
ROW_UNIT = 256


def _attn_units(q_ref, kt_ref, v_ref, o_ref, row_units, k_lo, n_keys, head_w, heads_per_vblock):
    heads_per_chunk = MXU_DIM // head_w
    n_rows = row_units[0].stop - row_units[0].start
    lane = lax.broadcasted_iota(jnp.int32, (n_rows, MXU_DIM), 1)
    lane_blk = lax.broadcasted_iota(jnp.int32, (n_rows, LANES), 1)
    units = [(rows, j) for rows in row_units for j in range(N_HEADS)]

    def scores(u):
        rows, j = units[u]
        ch = j // heads_per_chunk
        lo = (j % heads_per_chunk) * head_w
        qc = q_ref[rows, ch * MXU_DIM:(ch + 1) * MXU_DIM]
        qj = jnp.where((lane >= lo) & (lane < lo + head_w), qc, jnp.zeros_like(qc))
        return _dot(qj, kt_ref[ch * MXU_DIM:(ch + 1) * MXU_DIM, k_lo:k_lo + n_keys])

    def softmax(s):
        s = s.astype(BF)
        return jnp.exp2(s - jnp.max(s, axis=-1, keepdims=True))

    even = {}

    def values(u, e):
        rows, j = units[u]
        vb = j // heads_per_vblock
        half = j % 2 if heads_per_vblock == 2 else 0
        o = _dot(e, v_ref[k_lo:k_lo + n_keys, vb * MXU_DIM:(vb + 1) * MXU_DIM])
        blk = o[:, half * LANES:(half + 1) * LANES]
        row_sum = jnp.sum(jnp.where(lane_blk == ONES_LANE, blk, 0.0), axis=-1, keepdims=True)
        blk = blk * (1.0 / row_sum)
        if j % 2 == 0:
            even[rows.start] = blk
        else:
            pair = jnp.where(lane_blk < MLA_V_DIM, even.pop(rows.start), pltpu.roll(blk, MLA_V_DIM, 1))
            o_ref[rows, (j // 2) * LANES:(j // 2 + 1) * LANES] = pair.astype(BF)

    n = len(units)
    s_q = {u: scores(u) for u in range(min(2, n))}
    e_q = {}
    for u in range(n):
        e_q[u] = softmax(s_q.pop(u))
        if u + 2 < n:
            s_q[u + 2] = scores(u + 2)
        if u >= 1:
            values(u - 1, e_q.pop(u - 1))
    values(n - 1, e_q.pop(n - 1))


def _attn_kernel(q_ref, kt_ref, v_ref, o_ref, *, head_w, heads_per_vblock, with_ctx):
    def latent():
        units = [slice(r * ROW_UNIT, (r + 1) * ROW_UNIT) for r in range(TQ // ROW_UNIT)]
        _attn_units(q_ref, kt_ref, v_ref, o_ref, units, 0, T_ALL, head_w, heads_per_vblock)

    if not with_ctx:
        latent()
        return
    pl.when(pl.program_id(1) > 0)(latent)

    @pl.when(pl.program_id(1) == 0)
    def _():
        _attn_units(q_ref, kt_ref, v_ref, o_ref, [slice(0, CTX_LEN)], SEQ, CTX_LEN, head_w, heads_per_vblock)


def _attention(q, kt, v, head_w, heads_per_vblock, with_ctx, name):
    qw = N_HEADS * head_w
    ow = N_HEADS * MLA_V_DIM
    n_rows = T_ALL if with_ctx else SEQ
    blk = _ctx_first(SEQ // TQ, with_ctx)
    return pl.pallas_call(
        functools.partial(_attn_kernel, head_w=head_w, heads_per_vblock=heads_per_vblock, with_ctx=with_ctx),
        grid=(BATCH, pl.cdiv(n_rows, TQ)),
        in_specs=[
            pl.BlockSpec((None, TQ, qw), lambda b, t: (b, blk(t), 0)),
            pl.BlockSpec((None, qw, T_ALL), lambda b, t: (b, 0, 0)),
            pl.BlockSpec((None, T_ALL, v.shape[-1]), lambda b, t: (b, 0, 0)),
        ],
        out_specs=pl.BlockSpec((None, TQ, ow), lambda b, t: (b, blk(t), 0)),
        out_shape=jax.ShapeDtypeStruct((BATCH, n_rows, ow), BF),
        compiler_params=pltpu.CompilerParams(
            dimension_semantics=("parallel", "arbitrary"), vmem_limit_bytes=VMEM_LIMIT_BYTES),
        name=name,
    )(q, kt, v)


def _mixout_kernel(x_ref, modx_ref, modc_ref, f_ref, am_ref, ag_ref, wg_ref, bg_ref,
                   wfo_ref, wmo_ref, wgo_ref, wo_ref, lng_ref, lnb_ref, o_ref):
    tm = x_ref.shape[0]

    def modulated(rows):
        is_ctx = _is_ctx_rows(pl.program_id(1), tm, rows)
        shift = _mod_row(is_ctx, modx_ref, modc_ref, 0)
        scale = _mod_row(is_ctx, modx_ref, modc_ref, 1)
        return (_layer_norm(x_ref[rows, :]) * (1.0 + scale) + shift).astype(BF)

    def mix_and_norm(rows, h):
        mix = None
        for i, (a_ref, w_ref) in enumerate(((f_ref, wfo_ref), (am_ref, wmo_ref), (ag_ref, wgo_ref))):
            z = _dot(h, wg_ref[:, i * D_MODEL:(i + 1) * D_MODEL]) + bg_ref[:, i * D_MODEL:(i + 1) * D_MODEL]
            term = _dot(a_ref[rows, :], w_ref[...]) * (1.0 / (1.0 + jnp.exp(-z)))
            mix = term if mix is None else mix + term
        y = _dot(mix.astype(BF), wo_ref[...])
        gate1 = _mod_row(_is_ctx_rows(pl.program_id(1), tm, rows), modx_ref, modc_ref, 2)
        o_ref[rows, :] = _layer_norm(ALPHA * x_ref[rows, :] + gate1 * y) * lng_ref[...] + lnb_ref[...]

    _pipelined(_sub_rows(tm), modulated, mix_and_norm)


def _token_grid(with_ctx):
    return (TM_ALL, T_ALL) if with_ctx else (TM_LAT, SEQ)


def _mixout(x_all, mods, f, am, ag, wg, bg, wfo, wmo, wgo, wo, lng, lnb, with_ctx):
    tm, n_rows = _token_grid(with_ctx)
    tok = lambda w: pl.BlockSpec((None, tm, w), lambda b, t: (b, t, 0))
    return pl.pallas_call(
        _mixout_kernel,
        grid=(BATCH, n_rows // tm),
        in_specs=[
            tok(D_MODEL),
            pl.BlockSpec((None, N_MOD, D_MODEL), lambda b, t: (b, 0, 0)),
            pl.BlockSpec((None, N_MOD, D_MODEL), lambda b, t: (BATCH, 0, 0)),
            tok(F_WIDTH), tok(MLA_WIDTH), tok(GQA_WIDTH),
            _full(wg.shape), _full(bg.shape), _full(wfo.shape), _full(wmo.shape), _full(wgo.shape),
            _full(wo.shape), _full(lng.shape), _full(lnb.shape),
        ],
        out_specs=tok(D_MODEL),
        out_shape=jax.ShapeDtypeStruct((BATCH, n_rows, D_MODEL), F32),
        compiler_params=pltpu.CompilerParams(
            dimension_semantics=("parallel", "parallel"), vmem_limit_bytes=VMEM_LIMIT_BYTES),
        name="mixout",
    )(x_all, mods, mods, f, am, ag, wg, bg, wfo, wmo, wgo, wo, lng, lnb)


def _mlp_kernel(x_ref, modx_ref, modc_ref, w1_ref, w2_ref, lng_ref, lnb_ref, o_ref):
    tm = x_ref.shape[0]

    def modulated(rows):
        is_ctx = _is_ctx_rows(pl.program_id(1), tm, rows)
        shift = _mod_row(is_ctx, modx_ref, modc_ref, 3)
        scale = _mod_row(is_ctx, modx_ref, modc_ref, 4)
        return (_layer_norm(x_ref[rows, :]) * (1.0 + scale) + shift).astype(BF)

    def mlp_and_norm(rows, h):
        y = None
        for c in range(D_FF // FF_CHUNK):
            u = jnp.maximum(_dot(h, w1_ref[:, c * FF_CHUNK:(c + 1) * FF_CHUNK]), 0.0)
            t = _dot((u * u).astype(BF), w2_ref[c * FF_CHUNK:(c + 1) * FF_CHUNK, :])
            y = t if y is None else y + t
        gate2 = _mod_row(_is_ctx_rows(pl.program_id(1), tm, rows), modx_ref, modc_ref, 5)
        o_ref[rows, :] = _layer_norm(ALPHA * x_ref[rows, :] + gate2 * y) * lng_ref[...] + lnb_ref[...]

    _pipelined(_sub_rows(tm), modulated, mlp_and_norm)


def _mlp(x_all, mods, w1, w2, lng, lnb, with_ctx):
    tm, n_rows = _token_grid(with_ctx)
    tok = pl.BlockSpec((None, tm, D_MODEL), lambda b, t: (b, t, 0))
    return pl.pallas_call(
        _mlp_kernel,
        grid=(BATCH, n_rows // tm),
        in_specs=[
            tok,
            pl.BlockSpec((None, N_MOD, D_MODEL), lambda b, t: (b, 0, 0)),
            pl.BlockSpec((None, N_MOD, D_MODEL), lambda b, t: (BATCH, 0, 0)),
            _full(w1.shape), _full(w2.shape), _full(lng.shape), _full(lnb.shape),
        ],
        out_specs=tok,
        out_shape=jax.ShapeDtypeStruct((BATCH, n_rows, D_MODEL), F32),
        compiler_params=pltpu.CompilerParams(
            dimension_semantics=("parallel", "parallel"), vmem_limit_bytes=VMEM_LIMIT_BYTES),
        name="mlp",
    )(x_all, mods, mods, w1, w2, lng, lnb)


def _pad_heads(w, n_heads, head_dim):
    lead = w.shape[:-1]
    w = w.reshape(*lead, n_heads, head_dim)
    w = jnp.pad(w, [(0, 0)] * len(lead) + [(0, 0), (0, LANES - head_dim)])
    return w.reshape(*lead, n_heads * LANES)


def kernel(x, c, ctx, c_ctx, w_ada, b_ada, w_in, b_gate, mla_q_g, mla_kv_g, w_uq, w_uk, w_uv,
           gqa_q_g, gqa_k_g, w_fo, w_mo, w_go, w_o, ln1_g, ln1_b, w1, w2, ln2_g, ln2_b):
    L = DEPTH
    cl_np, sl_np = _dft_tables(SEQ)
    cc_np, sc_np = _dft_tables(CTX_LEN)
    c128_np, s128_np = _dft_tables(F_GROUP_DIM)
    e_kr_np, e_rep_np, e_val_np = _placement_tables()
    cl, sl, cc_t, sc_t = (jnp.asarray(a).astype(BF) for a in (cl_np, sl_np, cc_np, sc_np))
    cs128 = jnp.asarray(np.concatenate([c128_np, -s128_np], axis=1)).astype(BF)
    e_rep_t = jnp.asarray(e_rep_np.T).astype(BF)
    e_val = jnp.asarray(e_val_np).astype(BF)
    one_m = jnp.asarray(_ones_lane(MLA_HEADS * LANES))
    one_g = jnp.asarray(_ones_lane(GQA_KV_HEADS * MXU_DIM))
    rope = jnp.asarray(_rope_tables())

    zc = lambda n: jnp.zeros((L, D_MODEL, n), w_in.dtype)
    o_kr, o_kg, o_vg = MLA_KV_RANK, MLA_KV_RANK + MLA_ROPE_DIM, MLA_KV_RANK + MLA_ROPE_DIM + GQA_KV_WIDTH
    o_f = KV_COLS
    o_cq = o_f + F_WIDTH
    o_qg = o_cq + MLA_Q_RANK
    o_gate = o_qg + GQA_WIDTH
    wa = jnp.concatenate([
        w_in[:, :, :o_kr], w_in[:, :, o_kg:o_vg], w_in[:, :, o_vg:o_f], w_in[:, :, o_f:o_cq],
        w_in[:, :, o_cq:o_qg], w_in[:, :, o_qg:o_gate],
        zc(KR_LANE), w_in[:, :, o_kr:o_kg], zc(LANES - KR_LANE - MLA_ROPE_DIM)], axis=-1).astype(BF)
    wg = w_in[:, :, o_gate:].astype(BF)
    wk = jnp.concatenate([
        _pad_heads(w_uk, MLA_HEADS, MLA_NOPE_DIM),
        jnp.broadcast_to(jnp.asarray(e_kr_np), (L,) + e_kr_np.shape)], axis=1).astype(BF)
    wkt = jnp.swapaxes(wk, 1, 2)
    wuq = _pad_heads(w_uq, MLA_HEADS, MLA_QK_DIM).astype(BF)
    wuv = _pad_heads(w_uv, MLA_HEADS, MLA_V_DIM).astype(BF)
    wfo, wmo, wgo, wo = (w.astype(BF) for w in (w_fo, w_mo, w_go, w_o))
    w1b, w2b = w1.astype(BF), w2.astype(BF)
    row = lambda a: a.reshape(L, 1, a.shape[-1])
    kvg, qg = row(mla_kv_g), row(mla_q_g)
    gqg = row(jnp.concatenate([gqa_q_g, gqa_q_g], axis=-1))
    gkg = row(jnp.concatenate([gqa_k_g, gqa_k_g], axis=-1))
    bg = row(b_gate)
    l1g, l1b, l2g, l2b = row(ln1_g), row(ln1_b), row(ln2_g), row(ln2_b)

    cc = jnp.concatenate([c, c_ctx[None, :], jnp.zeros((MOD_ROWS - BATCH - 1, D_MODEL), c.dtype)], axis=0)
    mods = _modulation(cc, w_ada, b_ada).reshape(L, MOD_ROWS, N_MOD, D_MODEL)

    x_all = jnp.concatenate([x, ctx], axis=1)
    for l in range(L):
        kmt, vm, kgt, vg, qm, qgq, uc, us = _proj(
            x_all, mods[l], rope, wa[l], wkt[l], wuv[l], wuq[l], e_rep_t, e_val, cs128,
            kvg[l], qg[l], gqg[l], gkg[l], one_m, one_g)
        with_ctx = l < L - 1
        f = _fourier(uc, us, cl, sl, cc_t, sc_t, with_ctx)
        am = _attention(qm, kmt, vm, MLA_HEAD_PAD, MXU_DIM // LANES, with_ctx, "attn_mla")
        ag = _attention(qgq, kgt, vg, GQA_HEAD_DIM, GQA_GROUP, with_ctx, "attn_gqa")
        x_all = _mixout(x_all, mods[l], f, am, ag, wg[l], bg[l], wfo[l], wmo[l], wgo[l], wo[l], l1g[l], l1b[l],
                        with_ctx)
        x_all = _mlp(x_all, mods[l], w1b[l], w2b[l], l2g[l], l2b[l], with_ctx)
    return x_all
```

```python
import functools
import math

import numpy as np
import jax
import jax.numpy as jnp
from jax import lax
from jax.experimental import pallas as pl
from jax.experimental.pallas import tpu as pltpu

D_MODEL = 1024
BATCH = 8
SEQ = 2048
DEPTH = 4
CTX_LEN = 256
GRID_W = 64
ROPE_THETA = 10000.0
EPS = 1e-6

F_GROUPS = 4
F_GROUP_DIM = 128
F_WIDTH = F_GROUPS * F_GROUP_DIM
MLA_HEADS = 8
MLA_Q_RANK = 256
MLA_KV_RANK = 256
MLA_NOPE_DIM = 64
MLA_ROPE_DIM = 32
MLA_QK_DIM = MLA_NOPE_DIM + MLA_ROPE_DIM
MLA_V_DIM = 64
MLA_WIDTH = MLA_HEADS * MLA_V_DIM
GQA_HEADS = 8
GQA_KV_HEADS = 2
GQA_GROUP = GQA_HEADS // GQA_KV_HEADS
GQA_HEAD_DIM = 64
GQA_WIDTH = GQA_HEADS * GQA_HEAD_DIM
GQA_KV_WIDTH = GQA_KV_HEADS * GQA_HEAD_DIM
N_BRANCHES = 3
D_FF = 4 * D_MODEL
N_MOD = 6
KV_COLS = MLA_KV_RANK + MLA_ROPE_DIM + 2 * GQA_KV_WIDTH

T_ALL = SEQ + CTX_LEN
ALPHA = (2.0 * DEPTH) ** 0.25
LOG2E = math.log2(math.e)

LANES = 128
MXU_DIM = 256
VMEM_LIMIT_BYTES = 56 * 1024 * 1024

MLA_HEAD_PAD = LANES
MOD_ROWS = 16

A_CKV, A_KG, A_VG, A_F, A_CQ, A_QG, A_KR = 0, 256, 384, 512, 1024, 1280, 1792
A_COLS = 1920
KR_LANE = 64

TM_ALL = 768
TM_LAT = 512
N_SUB = 4
PROJ_SUB = 256
ONES_ROW = 64
TQ = 512
TR = 512
FF_CHUNK = 1024

BF = jnp.bfloat16
F32 = jnp.float32


def _dft_tables(n):
    k = np.arange(n, dtype=np.int64)
    ang = 2.0 * np.pi * ((k[:, None] * k[None, :]) % n).astype(np.float64) / n
    return np.cos(ang).astype(np.float32), (-np.sin(ang)).astype(np.float32)


def _rope_tables():
    t = np.arange(SEQ)
    rows = (t // GRID_W).astype(np.float64)
    cols = (t % GRID_W).astype(np.float64)

    def angles(d_rot):
        n = d_rot // 4
        freqs = ROPE_THETA ** (-np.arange(n, dtype=np.float64) / n)
        return np.concatenate([rows[:, None] * freqs, cols[:, None] * freqs], axis=-1)

    out = np.zeros((6, T_ALL, LANES), np.float64)
    out[0] = 1.0
    out[3] = 1.0
    a = angles(MLA_ROPE_DIM)
    h = MLA_ROPE_DIM // 2
    b0 = MLA_NOPE_DIM
    out[0, :SEQ, b0:b0 + h] = np.cos(a)
    out[0, :SEQ, b0 + h:b0 + 2 * h] = np.cos(a)
    out[1, :SEQ, b0 + h:b0 + 2 * h] = np.sin(a)
    out[2, :SEQ, b0:b0 + h] = -np.sin(a)
    a = angles(GQA_HEAD_DIM)
    h = GQA_HEAD_DIM // 2
    for b0 in (0, GQA_HEAD_DIM):
        out[3, :SEQ, b0:b0 + h] = np.cos(a)
        out[3, :SEQ, b0 + h:b0 + 2 * h] = np.cos(a)
        out[4, :SEQ, b0 + h:b0 + 2 * h] = np.sin(a)
        out[5, :SEQ, b0:b0 + h] = -np.sin(a)
    return out.astype(np.float32)


def _placement_tables():
    e_kr = np.zeros((LANES, MLA_HEADS * MLA_HEAD_PAD), np.float32)
    for hh in range(MLA_HEADS):
        for d in range(MLA_ROPE_DIM):
            e_kr[KR_LANE + d, hh * MLA_HEAD_PAD + MLA_NOPE_DIM + d] = 1.0
    e_rep = np.zeros((GQA_KV_WIDTH, GQA_WIDTH), np.float32)
    for g in range(GQA_KV_HEADS):
        for j in range(GQA_GROUP):
            for d in range(GQA_HEAD_DIM):
                e_rep[g * GQA_HEAD_DIM + d, (g * GQA_GROUP + j) * GQA_HEAD_DIM + d] = 1.0
    e_val_t = np.zeros((GQA_KV_HEADS * LANES, GQA_KV_WIDTH), np.float32)
    for g in range(GQA_KV_HEADS):
        for d in range(GQA_HEAD_DIM):
            e_val_t[g * LANES + d, g * GQA_HEAD_DIM + d] = 1.0
    return e_kr, e_rep, e_val_t


def _ones_rows(n_rows):
    v = np.zeros((n_rows, PROJ_SUB), np.float32)
    v[ONES_ROW::LANES, :] = 1.0
    return v


def _layer_norm(x):
    mu = jnp.mean(x, axis=-1, keepdims=True)
    xc = x - mu
    var = jnp.mean(xc * xc, axis=-1, keepdims=True)
    return xc * lax.rsqrt(var + EPS)


def _rms(x):
    return x * lax.rsqrt(jnp.mean(x * x, axis=-1, keepdims=True) + EPS)


def _sub_rows(tm, n_sub=N_SUB):
    sub = tm // n_sub
    return [slice(i * sub, (i + 1) * sub) for i in range(n_sub)]


def _pipelined(subs, first, second):
    nxt = first(subs[0])
    for i, rows in enumerate(subs):
        cur = nxt
        if i + 1 < len(subs):
            nxt = first(subs[i + 1])
        second(rows, cur)


def _is_ctx_rows(tile_idx, tm, rows):
    n = rows.stop - rows.start
    row = tile_idx * tm + rows.start + lax.broadcasted_iota(jnp.int32, (n, 1), 0)
    return row >= SEQ


def _mod_row(is_ctx, modx_ref, modc_ref, i):
    return jnp.where(is_ctx, modc_ref[i:i + 1, :], modx_ref[i:i + 1, :])


def _dot(a, b):
    return jnp.dot(a, b, preferred_element_type=F32)


def _rope(x, cos, sin_fwd, sin_bwd, half):
    return x * cos + pltpu.roll(x, half, 1) * sin_fwd + pltpu.roll(x, LANES - half, 1) * sin_bwd


def _head_rms_pair(x, gain):
    lane = lax.broadcasted_iota(jnp.int32, x.shape, 1)
    first = lane < GQA_HEAD_DIM
    sq = x * x
    s0 = jnp.sum(jnp.where(first, sq, 0.0), axis=-1, keepdims=True)
    s1 = jnp.sum(jnp.where(first, 0.0, sq), axis=-1, keepdims=True)
    ms = jnp.where(first, s0, s1) * (1.0 / GQA_HEAD_DIM)
    return x * lax.rsqrt(ms + EPS) * gain


def _mod_kernel(cc_ref, w_ref, b_ref, o_ref):
    cc = cc_ref[...]
    s = cc * (1.0 / (1.0 + jnp.exp(-cc)))
    o_ref[...] = jnp.dot(s, w_ref[...], preferred_element_type=F32,
                         precision=lax.Precision.HIGHEST) + b_ref[...]


def _modulation(cc, w_ada, b_ada):
    nj = N_MOD
    return pl.pallas_call(
        _mod_kernel,
        grid=(DEPTH, nj),
        in_specs=[
            pl.BlockSpec((MOD_ROWS, D_MODEL), lambda l, j: (0, 0)),
            pl.BlockSpec((None, D_MODEL, D_MODEL), lambda l, j: (l, 0, j)),
            pl.BlockSpec((None, 1, D_MODEL), lambda l, j: (l, 0, j)),
        ],
        out_specs=pl.BlockSpec((None, MOD_ROWS, D_MODEL), lambda l, j: (l, 0, j)),
        out_shape=jax.ShapeDtypeStruct((DEPTH, MOD_ROWS, N_MOD * D_MODEL), F32),
        compiler_params=pltpu.CompilerParams(
            dimension_semantics=("arbitrary", "arbitrary"), vmem_limit_bytes=VMEM_LIMIT_BYTES),
        name="modulation",
    )(cc, w_ada, b_ada.reshape(DEPTH, 1, N_MOD * D_MODEL))


def _dot_nt(a, b):
    return lax.dot_general(a, b, (((1,), (1,)), ((), ())), preferred_element_type=F32)


def _proj_kernel(x_ref, modx_ref, modc_ref, rope_ref, wa_ref, wk_ref, wuvt_ref, wuq_ref,
                 erep_ref, evalt_ref, cs_ref, kvg_ref, qg_ref, gqg_ref, gkg_ref, onem_ref, oneg_ref,
                 km_ref, vmt_ref, kg_ref, vgt_ref, qm_ref, qgo_ref, uc_ref, us_ref):
    tm = x_ref.shape[0]

    def project(rows):
        is_ctx = _is_ctx_rows(pl.program_id(1), tm, rows)
        shift = _mod_row(is_ctx, modx_ref, modc_ref, 0)
        scale = _mod_row(is_ctx, modx_ref, modc_ref, 1)
        h = (_layer_norm(x_ref[rows, :]) * (1.0 + scale) + shift).astype(BF)
        return _dot(h, wa_ref[...])

    def branches(rows, p):
        cos_m, sf_m, sb_m = rope_ref[0, rows, :], rope_ref[1, rows, :], rope_ref[2, rows, :]
        cos_g, sf_g, sb_g = rope_ref[3, rows, :], rope_ref[4, rows, :], rope_ref[5, rows, :]

        ckv = (_rms(p[:, A_CKV:A_CKV + MLA_KV_RANK]) * kvg_ref[...]).astype(BF)
        kr = _rope(p[:, A_KR:A_KR + LANES], cos_m, sf_m, sb_m, MLA_ROPE_DIM // 2).astype(BF)
        km_ref[rows, :] = _dot(jnp.concatenate([ckv, kr], axis=-1), wk_ref[...]).astype(BF)
        vmt_ref[:, rows] = (_dot_nt(wuvt_ref[...], ckv) + onem_ref[...]).astype(BF)

        kg = _rope(_head_rms_pair(p[:, A_KG:A_KG + LANES], gkg_ref[...]), cos_g, sf_g, sb_g, GQA_HEAD_DIM // 2)
        kg_ref[rows, :] = _dot(kg.astype(BF), erep_ref[...]).astype(BF)
        vgt_ref[:, rows] = (_dot_nt(evalt_ref[...], p[:, A_VG:A_VG + LANES].astype(BF)) + oneg_ref[...]).astype(BF)

        cq = (_rms(p[:, A_CQ:A_CQ + MLA_Q_RANK]) * qg_ref[...]).astype(BF)
        qm = _dot(cq, wuq_ref[...])
        sm = MLA_QK_DIM ** -0.5 * LOG2E
        for hh in range(MLA_HEADS):
            blk = qm[:, hh * LANES:(hh + 1) * LANES]
            qm_ref[rows, hh * LANES:(hh + 1) * LANES] = (
                _rope(blk, cos_m, sf_m, sb_m, MLA_ROPE_DIM // 2) * sm).astype(BF)

        sg = GQA_HEAD_DIM ** -0.5 * LOG2E
        for bb in range(GQA_WIDTH // LANES):
            blk = p[:, A_QG + bb * LANES:A_QG + (bb + 1) * LANES]
            blk = _rope(_head_rms_pair(blk, gqg_ref[...]), cos_g, sf_g, sb_g, GQA_HEAD_DIM // 2)
            qgo_ref[rows, bb * LANES:(bb + 1) * LANES] = (blk * sg).astype(BF)

        for g in range(F_GROUPS):
            fg = p[:, A_F + g * LANES:A_F + (g + 1) * LANES].astype(BF)
            r = _dot(fg, cs_ref[...])
            uc_ref[rows, g * LANES:(g + 1) * LANES] = r[:, :LANES].astype(BF)
            us_ref[rows, g * LANES:(g + 1) * LANES] = r[:, LANES:].astype(BF)

    _pipelined(_sub_rows(tm, tm // PROJ_SUB), project, branches)


def _full(shape):
    zeros = (0,) * len(shape)
    return pl.BlockSpec(shape, lambda *_: zeros, pipeline_mode=pl.Buffered(1))


def _layer_block(arr, l):
    idx = (l,) + (0,) * (arr.ndim - 1)
    return pl.BlockSpec((None,) + arr.shape[1:], lambda *_: idx, pipeline_mode=pl.Buffered(1))


def _mod_specs(l):
    return [pl.BlockSpec((None, None, N_MOD, D_MODEL), lambda b, t: (l, b, 0, 0)),
            pl.BlockSpec((None, None, N_MOD, D_MODEL), lambda b, t: (l, BATCH, 0, 0))]


def _proj(l, x_all, mods, rope, wa, wk, wuvt, wuq, erep, e_val_t, cs128, kvg, qg, gqg, gkg, onem, oneg):
    tm = TM_ALL
    lay = functools.partial(_layer_block, l=l)
    tok = lambda w: pl.BlockSpec((None, tm, w), lambda b, t: (b, t, 0))
    tok_t = lambda w: pl.BlockSpec((None, w, tm), lambda b, t: (b, 0, t))
    outs = ((MLA_HEADS * LANES, False), (MLA_HEADS * LANES, True), (GQA_WIDTH, False),
            (GQA_KV_HEADS * LANES, True), (MLA_HEADS * LANES, False), (GQA_WIDTH, False),
            (F_WIDTH, False), (F_WIDTH, False))
    return pl.pallas_call(
        _proj_kernel,
        grid=(BATCH, T_ALL // tm),
        in_specs=[
            tok(D_MODEL),
            *_mod_specs(l),
            pl.BlockSpec((6, tm, LANES), lambda b, t: (0, t, 0)),
            lay(wa), lay(wk), lay(wuvt), lay(wuq),
            _full(erep.shape), _full(e_val_t.shape), _full(cs128.shape),
            lay(kvg), lay(qg), lay(gqg), lay(gkg),
            _full(onem.shape), _full(oneg.shape),
        ],
        out_specs=[tok_t(w) if tr else tok(w) for w, tr in outs],
        out_shape=[jax.ShapeDtypeStruct((BATCH, w, T_ALL) if tr else (BATCH, T_ALL, w), BF) for w, tr in outs],
        compiler_params=pltpu.CompilerParams(
            dimension_semantics=("parallel", "parallel"), vmem_limit_bytes=VMEM_LIMIT_BYTES),
        name="proj",
    )(x_all, mods, mods, rope, wa, wk, wuvt, wuq, erep, e_val_t, cs128, kvg, qg, gqg, gkg, onem, oneg)


def _ctx_first(n_lat, with_ctx):
    if not with_ctx:
        return lambda t: t
    return lambda t: jnp.where(t == 0, n_lat, t - 1)


def _fourier_kernel(uc_ref, us_ref, cl_ref, sl_ref, cc_ref, sc_ref, o_ref, *, with_ctx):
    def latent():
        y = _dot(cl_ref[...], uc_ref[:SEQ, :]) + _dot(sl_ref[...], us_ref[:SEQ, :])
        o_ref[...] = (y * (SEQ * F_GROUP_DIM) ** -0.5).astype(BF)

    if not with_ctx:
        latent()
        return
    pl.when(pl.program_id(1) > 0)(latent)

    @pl.when(pl.program_id(1) == 0)
    def _():
        y = _dot(cc_ref[...], uc_ref[SEQ:, :]) + _dot(sc_ref[...], us_ref[SEQ:, :])
        o_ref[:CTX_LEN, :] = (y * (CTX_LEN * F_GROUP_DIM) ** -0.5).astype(BF)


def _fourier(uc, us, cl, sl, cc, sc, with_ctx):
    n_lat = SEQ // TR
    n_rows = T_ALL if with_ctx else SEQ
    blk = _ctx_first(n_lat, with_ctx)
    whole = pl.BlockSpec((None, T_ALL, F_WIDTH), lambda b, r: (b, 0, 0))
    tab = pl.BlockSpec((TR, SEQ), lambda b, r: (jnp.maximum(r - 1, 0) if with_ctx else r, 0))
    return pl.pallas_call(
        functools.partial(_fourier_kernel, with_ctx=with_ctx),
        grid=(BATCH, pl.cdiv(n_rows, TR)),
        in_specs=[whole, whole, tab, tab, _full(cc.shape), _full(sc.shape)],
        out_specs=pl.BlockSpec((None, TR, F_WIDTH), lambda b, r: (b, blk(r), 0)),
        out_shape=jax.ShapeDtypeStruct((BATCH, n_rows, F_WIDTH), BF),
        compiler_params=pltpu.CompilerParams(
            dimension_semantics=("parallel", "arbitrary"), vmem_limit_bytes=VMEM_LIMIT_BYTES),
        name="fourier",
    )(uc, us, cl, sl, cc, sc)


N_HEADS = 8
ROW_UNIT = 256


def _attn_units(q_ref, k_ref, vt_ref, o_ref, row_units, k_lo, n_keys, head_w, heads_per_vblock):
    heads_per_chunk = MXU_DIM // head_w
    n_rows = row_units[0].stop - row_units[0].start
    lane = lax.broadcasted_iota(jnp.int32, (n_rows, MXU_DIM), 1)
    lane_blk = lax.broadcasted_iota(jnp.int32, (n_rows, LANES), 1)
    units = [(rows, j) for rows in row_units for j in range(N_HEADS)]

    def scores(u):
        rows, j = units[u]
        ch = j // heads_per_chunk
        lo = (j % heads_per_chunk) * head_w
        qc = q_ref[rows, ch * MXU_DIM:(ch + 1) * MXU_DIM]
        qj = jnp.where((lane >= lo) & (lane < lo + head_w), qc, jnp.zeros_like(qc))
        return _dot_nt(k_ref[k_lo:k_lo + n_keys, ch * MXU_DIM:(ch + 1) * MXU_DIM], qj)

    def softmax(s):
        s = s.astype(BF)
        return jnp.exp2(s - jnp.max(s, axis=0, keepdims=True))

    even = {}

    def values(u, e):
        rows, j = units[u]
        r0 = (j // heads_per_vblock) * LANES
        ot = _dot(vt_ref[r0:r0 + LANES, k_lo:k_lo + n_keys], e)
        blk = (ot * (1.0 / ot[ONES_ROW:ONES_ROW + 1, :])).T
        if j % 2 == 0:
            even[rows.start] = blk
        else:
            pair = jnp.where(lane_blk < MLA_V_DIM, even.pop(rows.start), pltpu.roll(blk, MLA_V_DIM, 1))
            o_ref[rows, (j // 2) * LANES:(j // 2 + 1) * LANES] = pair.astype(BF)

    n = len(units)
    s_q = {u: scores(u) for u in range(min(2, n))}
    e_q = {}
    for u in range(n):
        e_q[u] = softmax(s_q.pop(u))
        if u + 2 < n:
            s_q[u + 2] = scores(u + 2)
        if u >= 1:
            values(u - 1, e_q.pop(u - 1))
    values(n - 1, e_q.pop(n - 1))


def _attn_kernel(q_ref, k_ref, vt_ref, o_ref, *, head_w, heads_per_vblock, with_ctx):
    def latent():
        units = [slice(r * ROW_UNIT, (r + 1) * ROW_UNIT) for r in range(TQ // ROW_UNIT)]
        _attn_units(q_ref, k_ref, vt_ref, o_ref, units, 0, T_ALL, head_w, heads_per_vblock)

    if not with_ctx:
        latent()
        return
    pl.when(pl.program_id(1) > 0)(latent)

    @pl.when(pl.program_id(1) == 0)
    def _():
        _attn_units(q_ref, k_ref, vt_ref, o_ref, [slice(0, CTX_LEN)], SEQ, CTX_LEN, head_w, heads_per_vblock)


def _attention(q, k, vt, head_w, heads_per_vblock, with_ctx, name):
    qw = N_HEADS * head_w
    ow = N_HEADS * MLA_V_DIM
    n_rows = T_ALL if with_ctx else SEQ
    blk = _ctx_first(SEQ // TQ, with_ctx)
    return pl.pallas_call(
        functools.partial(_attn_kernel, head_w=head_w, heads_per_vblock=heads_per_vblock, with_ctx=with_ctx),
        grid=(BATCH, pl.cdiv(n_rows, TQ)),
        in_specs=[
            pl.BlockSpec((None, TQ, qw), lambda b, t: (b, blk(t), 0)),
            pl.BlockSpec((None, T_ALL, qw), lambda b, t: (b, 0, 0)),
            pl.BlockSpec((None, vt.shape[1], T_ALL), lambda b, t: (b, 0, 0)),
        ],
        out_specs=pl.BlockSpec((None, TQ, ow), lambda b, t: (b, blk(t), 0)),
        out_shape=jax.ShapeDtypeStruct((BATCH, n_rows, ow), BF),
        compiler_params=pltpu.CompilerParams(
            dimension_semantics=("parallel", "arbitrary"), vmem_limit_bytes=VMEM_LIMIT_BYTES),
        name=name,
    )(q, k, vt)


def _mixout_kernel(x_ref, modx_ref, modc_ref, f_ref, am_ref, ag_ref, wg_ref, bg_ref,
                   wfo_ref, wmo_ref, wgo_ref, wo_ref, lng_ref, lnb_ref, o_ref):
    tm = x_ref.shape[0]

    def modulated(rows):
        is_ctx = _is_ctx_rows(pl.program_id(1), tm, rows)
        shift = _mod_row(is_ctx, modx_ref, modc_ref, 0)
        scale = _mod_row(is_ctx, modx_ref, modc_ref, 1)
        return (_layer_norm(x_ref[rows, :]) * (1.0 + scale) + shift).astype(BF)

    def mix_and_norm(rows, h):
        mix = None
        for i, (a_ref, w_ref) in enumerate(((f_ref, wfo_ref), (am_ref, wmo_ref), (ag_ref, wgo_ref))):
            z = _dot(h, wg_ref[:, i * D_MODEL:(i + 1) * D_MODEL]) + bg_ref[:, i * D_MODEL:(i + 1) * D_MODEL]
            term = _dot(a_ref[rows, :], w_ref[...]) * (1.0 / (1.0 + jnp.exp(-z)))
            mix = term if mix is None else mix + term
        y = _dot(mix.astype(BF), wo_ref[...])
        gate1 = _mod_row(_is_ctx_rows(pl.program_id(1), tm, rows), modx_ref, modc_ref, 2)
        o_ref[rows, :] = _layer_norm(ALPHA * x_ref[rows, :] + gate1 * y) * lng_ref[...] + lnb_ref[...]

    _pipelined(_sub_rows(tm), modulated, mix_and_norm)


def _token_grid(with_ctx):
    return (TM_ALL, T_ALL) if with_ctx else (TM_LAT, SEQ)


def _mixout(l, x_all, mods, f, am, ag, wg, bg, wfo, wmo, wgo, wo, lng, lnb, with_ctx):
    tm, n_rows = _token_grid(with_ctx)
    tok = lambda w: pl.BlockSpec((None, tm, w), lambda b, t: (b, t, 0))
    lay = functools.partial(_layer_block, l=l)
    return pl.pallas_call(
        _mixout_kernel,
        grid=(BATCH, n_rows // tm),
        in_specs=[
            tok(D_MODEL),
            *_mod_specs(l),
            tok(F_WIDTH), tok(MLA_WIDTH), tok(GQA_WIDTH),
            lay(wg), lay(bg), lay(wfo), lay(wmo), lay(wgo), lay(wo), lay(lng), lay(lnb),
        ],
        out_specs=tok(D_MODEL),
        out_shape=jax.ShapeDtypeStruct((BATCH, n_rows, D_MODEL), F32),
        compiler_params=pltpu.CompilerParams(
            dimension_semantics=("parallel", "parallel"), vmem_limit_bytes=VMEM_LIMIT_BYTES),
        name="mixout",
    )(x_all, mods, mods, f, am, ag, wg, bg, wfo, wmo, wgo, wo, lng, lnb)


def _mlp_kernel(x_ref, modx_ref, modc_ref, w1_ref, w2_ref, lng_ref, lnb_ref, o_ref):
    tm = x_ref.shape[0]

    def modulated(rows):
        is_ctx = _is_ctx_rows(pl.program_id(1), tm, rows)
        shift = _mod_row(is_ctx, modx_ref, modc_ref, 3)
        scale = _mod_row(is_ctx, modx_ref, modc_ref, 4)
        return (_layer_norm(x_ref[rows, :]) * (1.0 + scale) + shift).astype(BF)

    def mlp_and_norm(rows, h):
        y = None
        for c in range(D_FF // FF_CHUNK):
            u = jnp.maximum(_dot(h, w1_ref[:, c * FF_CHUNK:(c + 1) * FF_CHUNK]), 0.0)
            t = _dot((u * u).astype(BF), w2_ref[c * FF_CHUNK:(c + 1) * FF_CHUNK, :])
            y = t if y is None else y + t
        gate2 = _mod_row(_is_ctx_rows(pl.program_id(1), tm, rows), modx_ref, modc_ref, 5)
        o_ref[rows, :] = _layer_norm(ALPHA * x_ref[rows, :] + gate2 * y) * lng_ref[...] + lnb_ref[...]

    _pipelined(_sub_rows(tm), modulated, mlp_and_norm)


def _mlp(l, x_all, mods, w1, w2, lng, lnb, with_ctx):
    tm, n_rows = _token_grid(with_ctx)
    tok = pl.BlockSpec((None, tm, D_MODEL), lambda b, t: (b, t, 0))
    lay = functools.partial(_layer_block, l=l)
    return pl.pallas_call(
        _mlp_kernel,
        grid=(BATCH, n_rows // tm),
        in_specs=[tok, *_mod_specs(l), lay(w1), lay(w2), lay(lng), lay(lnb)],
        out_specs=tok,
        out_shape=jax.ShapeDtypeStruct((BATCH, n_rows, D_MODEL), F32),
        compiler_params=pltpu.CompilerParams(
            dimension_semantics=("parallel", "parallel"), vmem_limit_bytes=VMEM_LIMIT_BYTES),
        name="mlp",
    )(x_all, mods, mods, w1, w2, lng, lnb)


def _pad_heads(w, n_heads, head_dim):
    lead = w.shape[:-1]
    w = w.reshape(*lead, n_heads, head_dim)
    w = jnp.pad(w, [(0, 0)] * len(lead) + [(0, 0), (0, LANES - head_dim)])
    return w.reshape(*lead, n_heads * LANES)


def kernel(x, c, ctx, c_ctx, w_ada, b_ada, w_in, b_gate, mla_q_g, mla_kv_g, w_uq, w_uk, w_uv,
           gqa_q_g, gqa_k_g, w_fo, w_mo, w_go, w_o, ln1_g, ln1_b, w1, w2, ln2_g, ln2_b):
    L = DEPTH
    cl_np, sl_np = _dft_tables(SEQ)
    cc_np, sc_np = _dft_tables(CTX_LEN)
    c128_np, s128_np = _dft_tables(F_GROUP_DIM)
    e_kr_np, e_rep_np, e_val_t_np = _placement_tables()
    cl, sl, cc_t, sc_t = (jnp.asarray(a).astype(BF) for a in (cl_np, sl_np, cc_np, sc_np))
    cs128 = jnp.asarray(np.concatenate([c128_np, -s128_np], axis=1)).astype(BF)
    e_rep = jnp.asarray(e_rep_np).astype(BF)
    e_val_t = jnp.asarray(e_val_t_np).astype(BF)
    one_m = jnp.asarray(_ones_rows(MLA_HEADS * LANES))
    one_g = jnp.asarray(_ones_rows(GQA_KV_HEADS * LANES))
    rope = jnp.asarray(_rope_tables())

    zc = lambda n: jnp.zeros((L, D_MODEL, n), w_in.dtype)
    o_kr, o_kg, o_vg = MLA_KV_RANK, MLA_KV_RANK + MLA_ROPE_DIM, MLA_KV_RANK + MLA_ROPE_DIM + GQA_KV_WIDTH
    o_f = KV_COLS
    o_cq = o_f + F_WIDTH
    o_qg = o_cq + MLA_Q_RANK
    o_gate = o_qg + GQA_WIDTH
    wa = jnp.concatenate([
        w_in[:, :, :o_kr], w_in[:, :, o_kg:o_vg], w_in[:, :, o_vg:o_f], w_in[:, :, o_f:o_cq],
        w_in[:, :, o_cq:o_qg], w_in[:, :, o_qg:o_gate],
        zc(KR_LANE), w_in[:, :, o_kr:o_kg], zc(LANES - KR_LANE - MLA_ROPE_DIM)], axis=-1).astype(BF)
    wg = w_in[:, :, o_gate:].astype(BF)
    wk = jnp.concatenate([
        _pad_heads(w_uk, MLA_HEADS, MLA_NOPE_DIM),
        jnp.broadcast_to(jnp.asarray(e_kr_np), (L,) + e_kr_np.shape)], axis=1).astype(BF)
    wuq = _pad_heads(w_uq, MLA_HEADS, MLA_QK_DIM).astype(BF)
    wuvt = jnp.swapaxes(_pad_heads(w_uv, MLA_HEADS, MLA_V_DIM), 1, 2).astype(BF)
    wfo, wmo, wgo, wo = (w.astype(BF) for w in (w_fo, w_mo, w_go, w_o))
    w1b, w2b = w1.astype(BF), w2.astype(BF)
    row = lambda a: a.reshape(L, 1, a.shape[-1])
    kvg, qg = row(mla_kv_g), row(mla_q_g)
    gqg = row(jnp.concatenate([gqa_q_g, gqa_q_g], axis=-1))
    gkg = row(jnp.concatenate([gqa_k_g, gqa_k_g], axis=-1))
    bg = row(b_gate)
    l1g, l1b, l2g, l2b = row(ln1_g), row(ln1_b), row(ln2_g), row(ln2_b)

    cc = jnp.concatenate([c, c_ctx[None, :], jnp.zeros((MOD_ROWS - BATCH - 1, D_MODEL), c.dtype)], axis=0)
    mods = _modulation(cc, w_ada, b_ada).reshape(L, MOD_ROWS, N_MOD, D_MODEL)

    x_all = jnp.concatenate([x, ctx], axis=1)
    for l in range(L):
        km, vmt, kg, vgt, qm, qgq, uc, us = _proj(
            l, x_all, mods, rope, wa, wk, wuvt, wuq, e_rep, e_val_t, cs128, kvg, qg, gqg, gkg, one_m, one_g)
        with_ctx = l < L - 1
        f = _fourier(uc, us, cl, sl, cc_t, sc_t, with_ctx)
        am = _attention(qm, km, vmt, MLA_HEAD_PAD, 1, with_ctx, "attn_mla")
        ag = _attention(qgq, kg, vgt, GQA_HEAD_DIM, GQA_GROUP, with_ctx, "attn_gqa")
        x_all = _mixout(l, x_all, mods, f, am, ag, wg, bg, wfo, wmo, wgo, wo, l1g, l1b, with_ctx)
        x_all = _mlp(l, x_all, mods, w1b, w2b, l2g, l2b, with_ctx)
    return x_all
```

```python
import functools
import math

import numpy as np
import jax
import jax.numpy as jnp
from jax import lax
from jax.experimental import pallas as pl
from jax.experimental.pallas import tpu as pltpu

D_MODEL = 1024
BATCH = 8
SEQ = 2048
DEPTH = 4
CTX_LEN = 256
GRID_W = 64
ROPE_THETA = 10000.0
EPS = 1e-6

F_GROUPS = 4
F_GROUP_DIM = 128
F_WIDTH = F_GROUPS * F_GROUP_DIM
MLA_HEADS = 8
MLA_Q_RANK = 256
MLA_KV_RANK = 256
MLA_NOPE_DIM = 64
MLA_ROPE_DIM = 32
MLA_QK_DIM = MLA_NOPE_DIM + MLA_ROPE_DIM
MLA_V_DIM = 64
MLA_WIDTH = MLA_HEADS * MLA_V_DIM
GQA_HEADS = 8
GQA_KV_HEADS = 2
GQA_GROUP = GQA_HEADS // GQA_KV_HEADS
GQA_HEAD_DIM = 64
GQA_WIDTH = GQA_HEADS * GQA_HEAD_DIM
GQA_KV_WIDTH = GQA_KV_HEADS * GQA_HEAD_DIM
N_BRANCHES = 3
D_FF = 4 * D_MODEL
N_MOD = 6
KV_COLS = MLA_KV_RANK + MLA_ROPE_DIM + 2 * GQA_KV_WIDTH

T_ALL = SEQ + CTX_LEN
ALPHA = (2.0 * DEPTH) ** 0.25
LOG2E = math.log2(math.e)

LANES = 128
MXU_DIM = 256
VMEM_LIMIT_BYTES = 56 * 1024 * 1024

MLA_HEAD_PAD = LANES
MOD_ROWS = 16

A_CKV, A_KG, A_VG, A_F, A_CQ, A_QG, A_KR = 0, 256, 384, 512, 1024, 1280, 1792
A_COLS = 1920
KR_LANE = 64

TM_ALL = 768
TM_LAT = 512
SUB_ROWS = 256
ONES_ROW = 64
TQ = 1024
TR = 512

BF = jnp.bfloat16
F32 = jnp.float32


def _dft_tables(n):
    k = np.arange(n, dtype=np.int64)
    ang = 2.0 * np.pi * ((k[:, None] * k[None, :]) % n).astype(np.float64) / n
    return np.cos(ang).astype(np.float32), (-np.sin(ang)).astype(np.float32)


def _rope_tables():
    t = np.arange(SEQ)
    rows = (t // GRID_W).astype(np.float64)
    cols = (t % GRID_W).astype(np.float64)

    def angles(d_rot):
        n = d_rot // 4
        freqs = ROPE_THETA ** (-np.arange(n, dtype=np.float64) / n)
        return np.concatenate([rows[:, None] * freqs, cols[:, None] * freqs], axis=-1)

    out = np.zeros((6, T_ALL, LANES), np.float64)
    out[0] = 1.0
    out[3] = 1.0
    a = angles(MLA_ROPE_DIM)
    h = MLA_ROPE_DIM // 2
    b0 = MLA_NOPE_DIM
    out[0, :SEQ, b0:b0 + h] = np.cos(a)
    out[0, :SEQ, b0 + h:b0 + 2 * h] = np.cos(a)
    out[1, :SEQ, b0 + h:b0 + 2 * h] = np.sin(a)
    out[2, :SEQ, b0:b0 + h] = -np.sin(a)
    a = angles(GQA_HEAD_DIM)
    h = GQA_HEAD_DIM // 2
    for b0 in (0, GQA_HEAD_DIM):
        out[3, :SEQ, b0:b0 + h] = np.cos(a)
        out[3, :SEQ, b0 + h:b0 + 2 * h] = np.cos(a)
        out[4, :SEQ, b0 + h:b0 + 2 * h] = np.sin(a)
        out[5, :SEQ, b0:b0 + h] = -np.sin(a)
    return out.astype(np.float32)


def _placement_tables():
    e_kr = np.zeros((LANES, MLA_HEADS * MLA_HEAD_PAD), np.float32)
    for hh in range(MLA_HEADS):
        for d in range(MLA_ROPE_DIM):
            e_kr[KR_LANE + d, hh * MLA_HEAD_PAD + MLA_NOPE_DIM + d] = 1.0
    e_rep = np.zeros((GQA_KV_WIDTH, GQA_WIDTH), np.float32)
    for g in range(GQA_KV_HEADS):
        for j in range(GQA_GROUP):
            for d in range(GQA_HEAD_DIM):
                e_rep[g * GQA_HEAD_DIM + d, (g * GQA_GROUP + j) * GQA_HEAD_DIM + d] = 1.0
    e_val_t = np.zeros((GQA_KV_HEADS * LANES, GQA_KV_WIDTH), np.float32)
    for g in range(GQA_KV_HEADS):
        for d in range(GQA_HEAD_DIM):
            e_val_t[g * LANES + d, g * GQA_HEAD_DIM + d] = 1.0
    return e_kr, e_rep, e_val_t


def _ones_rows(n_rows):
    v = np.zeros((n_rows, SUB_ROWS), np.float32)
    v[ONES_ROW::LANES, :] = 1.0
    return v


def _layer_norm(x):
    mu = jnp.mean(x, axis=-1, keepdims=True)
    xc = x - mu
    var = jnp.mean(xc * xc, axis=-1, keepdims=True)
    return xc * lax.rsqrt(var + EPS)


def _rms(x):
    return x * lax.rsqrt(jnp.mean(x * x, axis=-1, keepdims=True) + EPS)


def _sub_rows(tm):
    return [slice(r, r + SUB_ROWS) for r in range(0, tm, SUB_ROWS)]


def _pipelined(subs, first, second):
    nxt = first(subs[0])
    for i, rows in enumerate(subs):
        cur = nxt
        if i + 1 < len(subs):
            nxt = first(subs[i + 1])
        second(rows, cur)


def _is_ctx_rows(tile_idx, tm, rows):
    return tile_idx * tm + rows.start >= SEQ


def _mod_row(is_ctx, modx_ref, modc_ref, i):
    return jnp.where(is_ctx, modc_ref[i:i + 1, :], modx_ref[i:i + 1, :])


def _dot(a, b):
    return jnp.dot(a, b, preferred_element_type=F32)


def _rope(x, cos, sin_fwd, sin_bwd, half):
    return x * cos + pltpu.roll(x, half, 1) * sin_fwd + pltpu.roll(x, LANES - half, 1) * sin_bwd


def _head_rms_pair(x, gain):
    lane = lax.broadcasted_iota(jnp.int32, x.shape, 1)
    first = lane < GQA_HEAD_DIM
    sq = x * x
    s0 = jnp.sum(jnp.where(first, sq, 0.0), axis=-1, keepdims=True)
    s1 = jnp.sum(jnp.where(first, 0.0, sq), axis=-1, keepdims=True)
    ms = jnp.where(first, s0, s1) * (1.0 / GQA_HEAD_DIM)
    return x * lax.rsqrt(ms + EPS) * gain


def _mod_kernel(cc_ref, w_ref, b_ref, o_ref):
    cc = cc_ref[...]
    s = cc * (1.0 / (1.0 + jnp.exp(-cc)))
    o_ref[...] = jnp.dot(s, w_ref[...], preferred_element_type=F32,
                         precision=lax.Precision.HIGHEST) + b_ref[...]


def _modulation(cc, w_ada, b_ada):
    nj = N_MOD
    return pl.pallas_call(
        _mod_kernel,
        grid=(DEPTH, nj),
        in_specs=[
            pl.BlockSpec((MOD_ROWS, D_MODEL), lambda l, j: (0, 0)),
            pl.BlockSpec((None, D_MODEL, D_MODEL), lambda l, j: (l, 0, j)),
            pl.BlockSpec((None, 1, D_MODEL), lambda l, j: (l, 0, j)),
        ],
        out_specs=pl.BlockSpec((None, MOD_ROWS, D_MODEL), lambda l, j: (l, 0, j)),
        out_shape=jax.ShapeDtypeStruct((DEPTH, MOD_ROWS, N_MOD * D_MODEL), F32),
        compiler_params=pltpu.CompilerParams(
            dimension_semantics=("arbitrary", "arbitrary"), vmem_limit_bytes=VMEM_LIMIT_BYTES),
        name="modulation",
    )(cc, w_ada, b_ada.reshape(DEPTH, 1, N_MOD * D_MODEL))


def _dot_nt(a, b):
    return lax.dot_general(a, b, (((1,), (1,)), ((), ())), preferred_element_type=F32)


def _proj_kernel(x_ref, modx_ref, modc_ref, rope_ref, wa_ref, wk_ref, wuvt_ref, wuq_ref,
                 erep_ref, evalt_ref, cs_ref, kvg_ref, qg_ref, gqg_ref, gkg_ref, onem_ref, oneg_ref,
                 km_ref, vmt_ref, kg_ref, vgt_ref, qm_ref, qgo_ref, uc_ref, us_ref):
    tm = x_ref.shape[0]

    def project(rows):
        is_ctx = _is_ctx_rows(pl.program_id(1), tm, rows)
        shift = _mod_row(is_ctx, modx_ref, modc_ref, 0)
        scale = _mod_row(is_ctx, modx_ref, modc_ref, 1)
        h = (_layer_norm(x_ref[rows, :]) * (1.0 + scale) + shift).astype(BF)
        return _dot(h, wa_ref[...])

    def branches(rows, p):
        cos_m, sf_m, sb_m = rope_ref[0, rows, :], rope_ref[1, rows, :], rope_ref[2, rows, :]
        cos_g, sf_g, sb_g = rope_ref[3, rows, :], rope_ref[4, rows, :], rope_ref[5, rows, :]

        ckv = (_rms(p[:, A_CKV:A_CKV + MLA_KV_RANK]) * kvg_ref[...]).astype(BF)
        kr = _rope(p[:, A_KR:A_KR + LANES], cos_m, sf_m, sb_m, MLA_ROPE_DIM // 2).astype(BF)
        km_ref[rows, :] = _dot(jnp.concatenate([ckv, kr], axis=-1), wk_ref[...]).astype(BF)
        vmt_ref[:, rows] = (_dot_nt(wuvt_ref[...], ckv) + onem_ref[...]).astype(BF)

        kg = _rope(_head_rms_pair(p[:, A_KG:A_KG + LANES], gkg_ref[...]), cos_g, sf_g, sb_g, GQA_HEAD_DIM // 2)
        kg_ref[rows, :] = _dot(kg.astype(BF), erep_ref[...]).astype(BF)
        vgt_ref[:, rows] = (_dot_nt(evalt_ref[...], p[:, A_VG:A_VG + LANES].astype(BF)) + oneg_ref[...]).astype(BF)

        cq = (_rms(p[:, A_CQ:A_CQ + MLA_Q_RANK]) * qg_ref[...]).astype(BF)
        qm = _dot(cq, wuq_ref[...])
        sm = MLA_QK_DIM ** -0.5 * LOG2E
        for hh in range(MLA_HEADS):
            blk = qm[:, hh * LANES:(hh + 1) * LANES]
            qm_ref[rows, hh * LANES:(hh + 1) * LANES] = (
                _rope(blk, cos_m, sf_m, sb_m, MLA_ROPE_DIM // 2) * sm).astype(BF)

        sg = GQA_HEAD_DIM ** -0.5 * LOG2E
        for bb in range(GQA_WIDTH // LANES):
            blk = p[:, A_QG + bb * LANES:A_QG + (bb + 1) * LANES]
            blk = _rope(_head_rms_pair(blk, gqg_ref[...]), cos_g, sf_g, sb_g, GQA_HEAD_DIM // 2)
            qgo_ref[rows, bb * LANES:(bb + 1) * LANES] = (blk * sg).astype(BF)

        for g in range(F_GROUPS):
            fg = p[:, A_F + g * LANES:A_F + (g + 1) * LANES].astype(BF)
            r = _dot(fg, cs_ref[...])
            uc_ref[rows, g * LANES:(g + 1) * LANES] = r[:, :LANES].astype(BF)
            us_ref[rows, g * LANES:(g + 1) * LANES] = r[:, LANES:].astype(BF)

    _pipelined(_sub_rows(tm), project, branches)


def _full(shape):
    zeros = (0,) * len(shape)
    return pl.BlockSpec(shape, lambda *_: zeros, pipeline_mode=pl.Buffered(1))


def _layer_block(arr, l):
    idx = (l,) + (0,) * (arr.ndim - 1)
    return pl.BlockSpec((None,) + arr.shape[1:], lambda *_: idx, pipeline_mode=pl.Buffered(1))


def _mod_specs(l):
    return [pl.BlockSpec((None, None, N_MOD, D_MODEL), lambda b, t: (l, b, 0, 0)),
            pl.BlockSpec((None, None, N_MOD, D_MODEL), lambda b, t: (l, BATCH, 0, 0))]


def _proj(l, x_all, mods, rope, wa, wk, wuvt, wuq, erep, e_val_t, cs128, kvg, qg, gqg, gkg, onem, oneg):
    tm = TM_ALL
    lay = functools.partial(_layer_block, l=l)
    tok = lambda w: pl.BlockSpec((None, tm, w), lambda b, t: (b, t, 0))
    tok_t = lambda w: pl.BlockSpec((None, w, tm), lambda b, t: (b, 0, t))
    outs = ((MLA_HEADS * LANES, False), (MLA_HEADS * LANES, True), (GQA_WIDTH, False),
            (GQA_KV_HEADS * LANES, True), (MLA_HEADS * LANES, False), (GQA_WIDTH, False),
            (F_WIDTH, False), (F_WIDTH, False))
    return pl.pallas_call(
        _proj_kernel,
        grid=(BATCH, T_ALL // tm),
        in_specs=[
            tok(D_MODEL),
            *_mod_specs(l),
            pl.BlockSpec((6, tm, LANES), lambda b, t: (0, t, 0)),
            lay(wa), lay(wk), lay(wuvt), lay(wuq),
            _full(erep.shape), _full(e_val_t.shape), _full(cs128.shape),
            lay(kvg), lay(qg), lay(gqg), lay(gkg),
            _full(onem.shape), _full(oneg.shape),
        ],
        out_specs=[tok_t(w) if tr else tok(w) for w, tr in outs],
        out_shape=[jax.ShapeDtypeStruct((BATCH, w, T_ALL) if tr else (BATCH, T_ALL, w), BF) for w, tr in outs],
        compiler_params=pltpu.CompilerParams(
            dimension_semantics=("parallel", "parallel"), vmem_limit_bytes=VMEM_LIMIT_BYTES),
        name="proj",
    )(x_all, mods, mods, rope, wa, wk, wuvt, wuq, erep, e_val_t, cs128, kvg, qg, gqg, gkg, onem, oneg)


def _ctx_first(n_lat, with_ctx):
    if not with_ctx:
        return lambda t: t
    return lambda t: jnp.where(t == 0, n_lat, t - 1)


def _fourier_kernel(uc_ref, us_ref, cl_ref, sl_ref, cc_ref, sc_ref, o_ref, *, with_ctx):
    def latent():
        y = _dot(cl_ref[...], uc_ref[:SEQ, :]) + _dot(sl_ref[...], us_ref[:SEQ, :])
        o_ref[...] = (y * (SEQ * F_GROUP_DIM) ** -0.5).astype(BF)

    if not with_ctx:
        latent()
        return
    pl.when(pl.program_id(1) > 0)(latent)

    @pl.when(pl.program_id(1) == 0)
    def _():
        y = _dot(cc_ref[...], uc_ref[SEQ:, :]) + _dot(sc_ref[...], us_ref[SEQ:, :])
        o_ref[:CTX_LEN, :] = (y * (CTX_LEN * F_GROUP_DIM) ** -0.5).astype(BF)


def _fourier(uc, us, cl, sl, cc, sc, with_ctx):
    n_lat = SEQ // TR
    n_rows = T_ALL if with_ctx else SEQ
    blk = _ctx_first(n_lat, with_ctx)
    whole = pl.BlockSpec((None, T_ALL, F_WIDTH), lambda b, r: (b, 0, 0))
    tab = pl.BlockSpec((TR, SEQ), lambda b, r: (jnp.maximum(r - 1, 0) if with_ctx else r, 0))
    return pl.pallas_call(
        functools.partial(_fourier_kernel, with_ctx=with_ctx),
        grid=(BATCH, pl.cdiv(n_rows, TR)),
        in_specs=[whole, whole, tab, tab, _full(cc.shape), _full(sc.shape)],
        out_specs=pl.BlockSpec((None, TR, F_WIDTH), lambda b, r: (b, blk(r), 0)),
        out_shape=jax.ShapeDtypeStruct((BATCH, n_rows, F_WIDTH), BF),
        compiler_params=pltpu.CompilerParams(
            dimension_semantics=("parallel", "arbitrary"), vmem_limit_bytes=VMEM_LIMIT_BYTES),
        name="fourier",
    )(uc, us, cl, sl, cc, sc)


N_HEADS = 8
ROW_UNIT = 256


def _attn_units(q_ref, k_ref, vt_ref, o_ref, row_units, k_lo, n_keys, head_w, heads_per_vblock):
    heads_per_chunk = MXU_DIM // head_w
    n_rows = row_units[0].stop - row_units[0].start
    lane = lax.broadcasted_iota(jnp.int32, (n_rows, MXU_DIM), 1)
    lane_blk = lax.broadcasted_iota(jnp.int32, (n_rows, LANES), 1)
    units = [(rows, j) for rows in row_units for j in range(N_HEADS)]

    def scores(u):
        rows, j = units[u]
        ch = j // heads_per_chunk
        lo = (j % heads_per_chunk) * head_w
        qc = q_ref[rows, ch * MXU_DIM:(ch + 1) * MXU_DIM]
        qj = jnp.where((lane >= lo) & (lane < lo + head_w), qc, jnp.zeros_like(qc))
        return _dot_nt(k_ref[k_lo:k_lo + n_keys, ch * MXU_DIM:(ch + 1) * MXU_DIM], qj)

    def softmax(s):
        s = s.astype(BF)
        return jnp.exp2(s - jnp.max(s, axis=0, keepdims=True))

    even = {}

    def values(u, e):
        rows, j = units[u]
        r0 = (j // heads_per_vblock) * LANES
        ot = _dot(vt_ref[r0:r0 + LANES, k_lo:k_lo + n_keys], e)
        blk = (ot * (1.0 / ot[ONES_ROW:ONES_ROW + 1, :])).T
        if j % 2 == 0:
            even[rows.start] = blk
        else:
            pair = jnp.where(lane_blk < MLA_V_DIM, even.pop(rows.start), pltpu.roll(blk, MLA_V_DIM, 1))
            o_ref[rows, (j // 2) * LANES:(j // 2 + 1) * LANES] = pair.astype(BF)

    n = len(units)
    s_q = {u: scores(u) for u in range(min(2, n))}
    e_q = {}
    for u in range(n):
        e_q[u] = softmax(s_q.pop(u))
        if u + 2 < n:
            s_q[u + 2] = scores(u + 2)
        if u >= 1:
            values(u - 1, e_q.pop(u - 1))
    values(n - 1, e_q.pop(n - 1))


def _attn_kernel(q_ref, k_ref, vt_ref, o_ref, *, head_w, heads_per_vblock, with_ctx):
    def latent():
        units = [slice(r * ROW_UNIT, (r + 1) * ROW_UNIT) for r in range(TQ // ROW_UNIT)]
        _attn_units(q_ref, k_ref, vt_ref, o_ref, units, 0, T_ALL, head_w, heads_per_vblock)

    if not with_ctx:
        latent()
        return
    pl.when(pl.program_id(1) > 0)(latent)

    @pl.when(pl.program_id(1) == 0)
    def _():
        _attn_units(q_ref, k_ref, vt_ref, o_ref, [slice(0, CTX_LEN)], SEQ, CTX_LEN, head_w, heads_per_vblock)


def _attention(q, k, vt, head_w, heads_per_vblock, with_ctx, name):
    qw = N_HEADS * head_w
    ow = N_HEADS * MLA_V_DIM
    n_rows = T_ALL if with_ctx else SEQ
    blk = _ctx_first(SEQ // TQ, with_ctx)
    return pl.pallas_call(
        functools.partial(_attn_kernel, head_w=head_w, heads_per_vblock=heads_per_vblock, with_ctx=with_ctx),
        grid=(BATCH, pl.cdiv(n_rows, TQ)),
        in_specs=[
            pl.BlockSpec((None, TQ, qw), lambda b, t: (b, blk(t), 0)),
            pl.BlockSpec((None, T_ALL, qw), lambda b, t: (b, 0, 0)),
            pl.BlockSpec((None, vt.shape[1], T_ALL), lambda b, t: (b, 0, 0)),
        ],
        out_specs=pl.BlockSpec((None, TQ, ow), lambda b, t: (b, blk(t), 0)),
        out_shape=jax.ShapeDtypeStruct((BATCH, n_rows, ow), BF),
        compiler_params=pltpu.CompilerParams(
            dimension_semantics=("parallel", "arbitrary"), vmem_limit_bytes=VMEM_LIMIT_BYTES),
        name=name,
    )(q, k, vt)


def _mixout_kernel(x_ref, modx_ref, modc_ref, f_ref, am_ref, ag_ref, wg_ref, bg_ref,
                   wfo_ref, wmo_ref, wgo_ref, wo_ref, lng_ref, lnb_ref, o_ref):
    tm = x_ref.shape[0]

    def modulated(rows):
        is_ctx = _is_ctx_rows(pl.program_id(1), tm, rows)
        shift = _mod_row(is_ctx, modx_ref, modc_ref, 0)
        scale = _mod_row(is_ctx, modx_ref, modc_ref, 1)
        return (_layer_norm(x_ref[rows, :]) * (1.0 + scale) + shift).astype(BF)

    def mix_and_norm(rows, h):
        mix = None
        for i, (a_ref, w_ref) in enumerate(((f_ref, wfo_ref), (am_ref, wmo_ref), (ag_ref, wgo_ref))):
            z = _dot(h, wg_ref[:, i * D_MODEL:(i + 1) * D_MODEL]) + bg_ref[:, i * D_MODEL:(i + 1) * D_MODEL]
            term = _dot(a_ref[rows, :], w_ref[...]) * (1.0 / (1.0 + jnp.exp(-z)))
            mix = term if mix is None else mix + term
        y = _dot(mix.astype(BF), wo_ref[...])
        gate1 = _mod_row(_is_ctx_rows(pl.program_id(1), tm, rows), modx_ref, modc_ref, 2)
        o_ref[rows, :] = _layer_norm(ALPHA * x_ref[rows, :] + gate1 * y) * lng_ref[...] + lnb_ref[...]

    _pipelined(_sub_rows(tm), modulated, mix_and_norm)


def _token_grid(with_ctx):
    return (TM_ALL, T_ALL) if with_ctx else (TM_LAT, SEQ)


def _mixout(l, x_all, mods, f, am, ag, wg, bg, wfo, wmo, wgo, wo, lng, lnb, with_ctx):
    tm, n_rows = _token_grid(with_ctx)
    tok = lambda w: pl.BlockSpec((None, tm, w), lambda b, t: (b, t, 0))
    lay = functools.partial(_layer_block, l=l)
    return pl.pallas_call(
        _mixout_kernel,
        grid=(BATCH, n_rows // tm),
        in_specs=[
            tok(D_MODEL),
            *_mod_specs(l),
            tok(F_WIDTH), tok(MLA_WIDTH), tok(GQA_WIDTH),
            lay(wg), lay(bg), lay(wfo), lay(wmo), lay(wgo), lay(wo), lay(lng), lay(lnb),
        ],
        out_specs=tok(D_MODEL),
        out_shape=jax.ShapeDtypeStruct((BATCH, n_rows, D_MODEL), F32),
        compiler_params=pltpu.CompilerParams(
            dimension_semantics=("parallel", "parallel"), vmem_limit_bytes=VMEM_LIMIT_BYTES),
        name="mixout",
    )(x_all, mods, mods, f, am, ag, wg, bg, wfo, wmo, wgo, wo, lng, lnb)


def _mlp_kernel(x_ref, modx_ref, modc_ref, w1_ref, w2_ref, lng_ref, lnb_ref, o_ref):
    tm = x_ref.shape[0]

    def modulated(rows):
        is_ctx = _is_ctx_rows(pl.program_id(1), tm, rows)
        shift = _mod_row(is_ctx, modx_ref, modc_ref, 3)
        scale = _mod_row(is_ctx, modx_ref, modc_ref, 4)
        return (_layer_norm(x_ref[rows, :]) * (1.0 + scale) + shift).astype(BF)

    def mlp_and_norm(rows, h):
        u = jnp.maximum(_dot(h, w1_ref[...]), 0.0)
        y = _dot((u * u).astype(BF), w2_ref[...])
        gate2 = _mod_row(_is_ctx_rows(pl.program_id(1), tm, rows), modx_ref, modc_ref, 5)
        o_ref[rows, :] = _layer_norm(ALPHA * x_ref[rows, :] + gate2 * y) * lng_ref[...] + lnb_ref[...]

    _pipelined(_sub_rows(tm), modulated, mlp_and_norm)


def _mlp(l, x_all, mods, w1, w2, lng, lnb, with_ctx):
    tm, n_rows = _token_grid(with_ctx)
    tok = pl.BlockSpec((None, tm, D_MODEL), lambda b, t: (b, t, 0))
    lay = functools.partial(_layer_block, l=l)
    return pl.pallas_call(
        _mlp_kernel,
        grid=(BATCH, n_rows // tm),
        in_specs=[tok, *_mod_specs(l), lay(w1), lay(w2), lay(lng), lay(lnb)],
        out_specs=tok,
        out_shape=jax.ShapeDtypeStruct((BATCH, n_rows, D_MODEL), F32),
        compiler_params=pltpu.CompilerParams(
            dimension_semantics=("parallel", "parallel"), vmem_limit_bytes=VMEM_LIMIT_BYTES),
        name="mlp",
    )(x_all, mods, mods, w1, w2, lng, lnb)


def _pad_heads(w, n_heads, head_dim):
    lead = w.shape[:-1]
    w = w.reshape(*lead, n_heads, head_dim)
    w = jnp.pad(w, [(0, 0)] * len(lead) + [(0, 0), (0, LANES - head_dim)])
    return w.reshape(*lead, n_heads * LANES)


def kernel(x, c, ctx, c_ctx, w_ada, b_ada, w_in, b_gate, mla_q_g, mla_kv_g, w_uq, w_uk, w_uv,
           gqa_q_g, gqa_k_g, w_fo, w_mo, w_go, w_o, ln1_g, ln1_b, w1, w2, ln2_g, ln2_b):
    L = DEPTH
    cl_np, sl_np = _dft_tables(SEQ)
    cc_np, sc_np = _dft_tables(CTX_LEN)
    c128_np, s128_np = _dft_tables(F_GROUP_DIM)
    e_kr_np, e_rep_np, e_val_t_np = _placement_tables()
    cl, sl, cc_t, sc_t = (jnp.asarray(a).astype(BF) for a in (cl_np, sl_np, cc_np, sc_np))
    cs128 = jnp.asarray(np.concatenate([c128_np, -s128_np], axis=1)).astype(BF)
    e_rep = jnp.asarray(e_rep_np).astype(BF)
    e_val_t = jnp.asarray(e_val_t_np).astype(BF)
    one_m = jnp.asarray(_ones_rows(MLA_HEADS * LANES))
    one_g = jnp.asarray(_ones_rows(GQA_KV_HEADS * LANES))
    rope = jnp.asarray(_rope_tables())

    zc = lambda n: jnp.zeros((L, D_MODEL, n), w_in.dtype)
    o_kr, o_kg, o_vg = MLA_KV_RANK, MLA_KV_RANK + MLA_ROPE_DIM, MLA_KV_RANK + MLA_ROPE_DIM + GQA_KV_WIDTH
    o_f = KV_COLS
    o_cq = o_f + F_WIDTH
    o_qg = o_cq + MLA_Q_RANK
    o_gate = o_qg + GQA_WIDTH
    wa = jnp.concatenate([
        w_in[:, :, :o_kr], w_in[:, :, o_kg:o_vg], w_in[:, :, o_vg:o_f], w_in[:, :, o_f:o_cq],
        w_in[:, :, o_cq:o_qg], w_in[:, :, o_qg:o_gate],
        zc(KR_LANE), w_in[:, :, o_kr:o_kg], zc(LANES - KR_LANE - MLA_ROPE_DIM)], axis=-1).astype(BF)
    wg = w_in[:, :, o_gate:].astype(BF)
    wk = jnp.concatenate([
        _pad_heads(w_uk, MLA_HEADS, MLA_NOPE_DIM),
        jnp.broadcast_to(jnp.asarray(e_kr_np), (L,) + e_kr_np.shape)], axis=1).astype(BF)
    wuq = _pad_heads(w_uq, MLA_HEADS, MLA_QK_DIM).astype(BF)
    wuvt = jnp.swapaxes(_pad_heads(w_uv, MLA_HEADS, MLA_V_DIM), 1, 2).astype(BF)
    wfo, wmo, wgo, wo = (w.astype(BF) for w in (w_fo, w_mo, w_go, w_o))
    w1b, w2b = w1.astype(BF), w2.astype(BF)
    row = lambda a: a.reshape(L, 1, a.shape[-1])
    kvg, qg = row(mla_kv_g), row(mla_q_g)
    gqg = row(jnp.concatenate([gqa_q_g, gqa_q_g], axis=-1))
    gkg = row(jnp.concatenate([gqa_k_g, gqa_k_g], axis=-1))
    bg = row(b_gate)
    l1g, l1b, l2g, l2b = row(ln1_g), row(ln1_b), row(ln2_g), row(ln2_b)

    cc = jnp.concatenate([c, c_ctx[None, :], jnp.zeros((MOD_ROWS - BATCH - 1, D_MODEL), c.dtype)], axis=0)
    mods = _modulation(cc, w_ada, b_ada).reshape(L, MOD_ROWS, N_MOD, D_MODEL)

    x_all = jnp.concatenate([x, ctx], axis=1)
    for l in range(L):
        km, vmt, kg, vgt, qm, qgq, uc, us = _proj(
            l, x_all, mods, rope, wa, wk, wuvt, wuq, e_rep, e_val_t, cs128, kvg, qg, gqg, gkg, one_m, one_g)
        with_ctx = l < L - 1
        f = _fourier(uc, us, cl, sl, cc_t, sc_t, with_ctx)
        am = _attention(qm, km, vmt, MLA_HEAD_PAD, 1, with_ctx, "attn_mla")
        ag = _attention(qgq, kg, vgt, GQA_HEAD_DIM, GQA_GROUP, with_ctx, "attn_gqa")
        x_all = _mixout(l, x_all, mods, f, am, ag, wg, bg, wfo, wmo, wgo, wo, l1g, l1b, with_ctx)
        x_all = _mlp(l, x_all, mods, w1b, w2b, l2g, l2b, with_ctx)
    return x_all
```

```python
import functools
import math

import numpy as np
import jax
import jax.numpy as jnp
from jax import lax
from jax.experimental import pallas as pl
from jax.experimental.pallas import tpu as pltpu

D_MODEL = 1024
BATCH = 8
SEQ = 2048
DEPTH = 4
CTX_LEN = 256
GRID_W = 64
ROPE_THETA = 10000.0
EPS = 1e-6

F_GROUPS = 4
F_GROUP_DIM = 128
F_WIDTH = F_GROUPS * F_GROUP_DIM
MLA_HEADS = 8
MLA_Q_RANK = 256
MLA_KV_RANK = 256
MLA_NOPE_DIM = 64
MLA_ROPE_DIM = 32
MLA_QK_DIM = MLA_NOPE_DIM + MLA_ROPE_DIM
MLA_V_DIM = 64
MLA_WIDTH = MLA_HEADS * MLA_V_DIM
GQA_HEADS = 8
GQA_KV_HEADS = 2
GQA_GROUP = GQA_HEADS // GQA_KV_HEADS
GQA_HEAD_DIM = 64
GQA_WIDTH = GQA_HEADS * GQA_HEAD_DIM
GQA_KV_WIDTH = GQA_KV_HEADS * GQA_HEAD_DIM
N_BRANCHES = 3
D_FF = 4 * D_MODEL
N_MOD = 6
KV_COLS = MLA_KV_RANK + MLA_ROPE_DIM + 2 * GQA_KV_WIDTH

T_ALL = SEQ + CTX_LEN
ALPHA = (2.0 * DEPTH) ** 0.25
LOG2E = math.log2(math.e)

LANES = 128
MXU_DIM = 256
VMEM_LIMIT_BYTES = 56 * 1024 * 1024

MLA_HEAD_PAD = LANES
MOD_ROWS = 16

A_CKV, A_KG, A_VG, A_F, A_CQ, A_QG, A_KR = 0, 256, 384, 512, 1024, 1280, 1792
A_COLS = 1920
KR_LANE = 64

TM_ALL = 768
TM_LAT = 512
SUB_ROWS = 256
ONES_ROW = 64
TQ = 1024
TR = 1024

BF = jnp.bfloat16
F32 = jnp.float32


def _dft_tables(n):
    k = np.arange(n, dtype=np.int64)
    ang = 2.0 * np.pi * ((k[:, None] * k[None, :]) % n).astype(np.float64) / n
    return np.cos(ang).astype(np.float32), (-np.sin(ang)).astype(np.float32)


def _rope_tables():
    t = np.arange(SEQ)
    rows = (t // GRID_W).astype(np.float64)
    cols = (t % GRID_W).astype(np.float64)

    def angles(d_rot):
        n = d_rot // 4
        freqs = ROPE_THETA ** (-np.arange(n, dtype=np.float64) / n)
        return np.concatenate([rows[:, None] * freqs, cols[:, None] * freqs], axis=-1)

    out = np.zeros((6, T_ALL, LANES), np.float64)
    out[0] = 1.0
    out[3] = 1.0
    a = angles(MLA_ROPE_DIM)
    h = MLA_ROPE_DIM // 2
    b0 = MLA_NOPE_DIM
    out[0, :SEQ, b0:b0 + h] = np.cos(a)
    out[0, :SEQ, b0 + h:b0 + 2 * h] = np.cos(a)
    out[1, :SEQ, b0 + h:b0 + 2 * h] = np.sin(a)
    out[2, :SEQ, b0:b0 + h] = -np.sin(a)
    a = angles(GQA_HEAD_DIM)
    h = GQA_HEAD_DIM // 2
    for b0 in (0, GQA_HEAD_DIM):
        out[3, :SEQ, b0:b0 + h] = np.cos(a)
        out[3, :SEQ, b0 + h:b0 + 2 * h] = np.cos(a)
        out[4, :SEQ, b0 + h:b0 + 2 * h] = np.sin(a)
        out[5, :SEQ, b0:b0 + h] = -np.sin(a)
    return out.astype(np.float32)


def _placement_tables():
    e_kr = np.zeros((LANES, MLA_HEADS * MLA_HEAD_PAD), np.float32)
    for hh in range(MLA_HEADS):
        for d in range(MLA_ROPE_DIM):
            e_kr[KR_LANE + d, hh * MLA_HEAD_PAD + MLA_NOPE_DIM + d] = 1.0
    e_rep = np.zeros((GQA_KV_WIDTH, GQA_WIDTH), np.float32)
    for g in range(GQA_KV_HEADS):
        for j in range(GQA_GROUP):
            for d in range(GQA_HEAD_DIM):
                e_rep[g * GQA_HEAD_DIM + d, (g * GQA_GROUP + j) * GQA_HEAD_DIM + d] = 1.0
    e_val_t = np.zeros((GQA_KV_HEADS * LANES, GQA_KV_WIDTH), np.float32)
    for g in range(GQA_KV_HEADS):
        for d in range(GQA_HEAD_DIM):
            e_val_t[g * LANES + d, g * GQA_HEAD_DIM + d] = 1.0
    return e_kr, e_rep, e_val_t


def _ones_rows(n_rows):
    v = np.zeros((n_rows, SUB_ROWS), np.float32)
    v[ONES_ROW::LANES, :] = 1.0
    return v


def _layer_norm(x):
    mu = jnp.mean(x, axis=-1, keepdims=True)
    xc = x - mu
    var = jnp.mean(xc * xc, axis=-1, keepdims=True)
    return xc * lax.rsqrt(var + EPS)


def _rms(x):
    return x * lax.rsqrt(jnp.mean(x * x, axis=-1, keepdims=True) + EPS)


def _sub_rows(tm):
    return [slice(r, r + SUB_ROWS) for r in range(0, tm, SUB_ROWS)]


def _pipelined(subs, first, second):
    nxt = first(subs[0])
    for i, rows in enumerate(subs):
        cur = nxt
        if i + 1 < len(subs):
            nxt = first(subs[i + 1])
        second(rows, cur)


def _is_ctx_rows(tile_idx, tm, rows):
    return tile_idx * tm + rows.start >= SEQ


def _token_reader(x_refs, tile_idx):
    if len(x_refs) == 1:
        return x_refs[0].shape[0], lambda rows: x_refs[0][rows, :]
    *lat, ctx = x_refs
    tm = len(lat) * SUB_ROWS

    def read(rows):
        v = lat[rows.start // SUB_ROWS][...]
        if rows.stop == tm:
            v = jnp.where(_is_ctx_rows(tile_idx, tm, rows), ctx[...], v)
        return v

    return tm, read


def _token_specs(x_parts, tm):
    if len(x_parts) == 1:
        return [pl.BlockSpec((None, tm, D_MODEL), lambda b, t: (b, t, 0))]
    assert CTX_LEN == SUB_ROWS and T_ALL % tm == 0
    n_sub, last = tm // SUB_ROWS, SEQ // SUB_ROWS - 1
    lat = [pl.BlockSpec((None, SUB_ROWS, D_MODEL), lambda b, t, i=i: (b, jnp.minimum(t * n_sub + i, last), 0))
           for i in range(n_sub)]
    return lat + [pl.BlockSpec((None, CTX_LEN, D_MODEL), lambda b, t: (b, 0, 0))]


def _token_operands(x_parts, tm):
    if len(x_parts) == 1:
        return list(x_parts)
    x, ctx = x_parts
    return [x] * (tm // SUB_ROWS) + [ctx]


def _mod_row(is_ctx, modx_ref, modc_ref, i):
    return jnp.where(is_ctx, modc_ref[i:i + 1, :], modx_ref[i:i + 1, :])


def _dot(a, b):
    return jnp.dot(a, b, preferred_element_type=F32)


def _rope(x, cos, sin_fwd, sin_bwd, half):
    return x * cos + pltpu.roll(x, half, 1) * sin_fwd + pltpu.roll(x, LANES - half, 1) * sin_bwd


def _head_rms_pair(x, gain):
    lane = lax.broadcasted_iota(jnp.int32, x.shape, 1)
    first = lane < GQA_HEAD_DIM
    sq = x * x
    s0 = jnp.sum(jnp.where(first, sq, 0.0), axis=-1, keepdims=True)
    s1 = jnp.sum(jnp.where(first, 0.0, sq), axis=-1, keepdims=True)
    ms = jnp.where(first, s0, s1) * (1.0 / GQA_HEAD_DIM)
    return x * lax.rsqrt(ms + EPS) * gain


def _split_bf16(a):
    hi = a.astype(BF)
    return hi, (a - hi.astype(F32)).astype(BF)


def _mod_kernel(cc_ref, w_ref, b_ref, o_ref):
    cc = cc_ref[...]
    s = cc * (1.0 / (1.0 + jnp.exp(-cc)))
    s_hi, s_lo = _split_bf16(s)
    w_hi, w_lo = _split_bf16(w_ref[...])
    acc = _dot(jnp.concatenate([s_hi, s_lo], axis=0), w_hi)
    o_ref[...] = acc[:MOD_ROWS] + acc[MOD_ROWS:] + _dot(s_hi, w_lo) + b_ref[...]


def _modulation(cc, w_ada, b_ada):
    nj = N_MOD
    return pl.pallas_call(
        _mod_kernel,
        grid=(DEPTH, nj),
        in_specs=[
            pl.BlockSpec((MOD_ROWS, D_MODEL), lambda l, j: (0, 0)),
            pl.BlockSpec((None, D_MODEL, D_MODEL), lambda l, j: (l, 0, j)),
            pl.BlockSpec((None, 1, D_MODEL), lambda l, j: (l, 0, j)),
        ],
        out_specs=pl.BlockSpec((None, MOD_ROWS, D_MODEL), lambda l, j: (l, 0, j)),
        out_shape=jax.ShapeDtypeStruct((DEPTH, MOD_ROWS, N_MOD * D_MODEL), F32),
        compiler_params=pltpu.CompilerParams(
            dimension_semantics=("arbitrary", "arbitrary"), vmem_limit_bytes=VMEM_LIMIT_BYTES),
        name="modulation",
    )(cc, w_ada, b_ada.reshape(DEPTH, 1, N_MOD * D_MODEL))


def _dot_nt(a, b):
    return lax.dot_general(a, b, (((1,), (1,)), ((), ())), preferred_element_type=F32)


def _proj_kernel(*refs, n_x):
    (modx_ref, modc_ref, rope_ref, wa_ref, wk_ref, wuvt_ref, wuq_ref,
     erep_ref, evalt_ref, cs_ref, kvg_ref, qg_ref, gqg_ref, gkg_ref, onem_ref, oneg_ref,
     km_ref, vmt_ref, kg_ref, vgt_ref, qm_ref, qgo_ref, uc_ref, us_ref) = refs[n_x:]
    tm, read_x = _token_reader(refs[:n_x], pl.program_id(1))

    def project(rows):
        is_ctx = _is_ctx_rows(pl.program_id(1), tm, rows)
        shift = _mod_row(is_ctx, modx_ref, modc_ref, 0)
        scale = _mod_row(is_ctx, modx_ref, modc_ref, 1)
        h = (_layer_norm(read_x(rows)) * (1.0 + scale) + shift).astype(BF)
        return _dot(h, wa_ref[...])

    def branches(rows, p):
        cos_m, sf_m, sb_m = rope_ref[0, rows, :], rope_ref[1, rows, :], rope_ref[2, rows, :]
        cos_g, sf_g, sb_g = rope_ref[3, rows, :], rope_ref[4, rows, :], rope_ref[5, rows, :]

        ckv = (_rms(p[:, A_CKV:A_CKV + MLA_KV_RANK]) * kvg_ref[...]).astype(BF)
        kr = _rope(p[:, A_KR:A_KR + LANES], cos_m, sf_m, sb_m, MLA_ROPE_DIM // 2).astype(BF)
        km_ref[rows, :] = _dot(jnp.concatenate([ckv, kr], axis=-1), wk_ref[...]).astype(BF)
        vmt_ref[:, rows] = (_dot_nt(wuvt_ref[...], ckv) + onem_ref[...]).astype(BF)

        kg = _rope(_head_rms_pair(p[:, A_KG:A_KG + LANES], gkg_ref[...]), cos_g, sf_g, sb_g, GQA_HEAD_DIM // 2)
        kg_ref[rows, :] = _dot(kg.astype(BF), erep_ref[...]).astype(BF)
        vgt_ref[:, rows] = (_dot_nt(evalt_ref[...], p[:, A_VG:A_VG + LANES].astype(BF)) + oneg_ref[...]).astype(BF)

        cq = (_rms(p[:, A_CQ:A_CQ + MLA_Q_RANK]) * qg_ref[...]).astype(BF)
        qm = _dot(cq, wuq_ref[...])
        sm = MLA_QK_DIM ** -0.5 * LOG2E
        for hh in range(MLA_HEADS):
            blk = qm[:, hh * LANES:(hh + 1) * LANES]
            qm_ref[rows, hh * LANES:(hh + 1) * LANES] = (
                _rope(blk, cos_m, sf_m, sb_m, MLA_ROPE_DIM // 2) * sm).astype(BF)

        sg = GQA_HEAD_DIM ** -0.5 * LOG2E
        for bb in range(GQA_WIDTH // LANES):
            blk = p[:, A_QG + bb * LANES:A_QG + (bb + 1) * LANES]
            blk = _rope(_head_rms_pair(blk, gqg_ref[...]), cos_g, sf_g, sb_g, GQA_HEAD_DIM // 2)
            qgo_ref[rows, bb * LANES:(bb + 1) * LANES] = (blk * sg).astype(BF)

        for g in range(F_GROUPS):
            fg = p[:, A_F + g * LANES:A_F + (g + 1) * LANES].astype(BF)
            r = _dot(fg, cs_ref[...])
            uc_ref[rows, g * LANES:(g + 1) * LANES] = r[:, :LANES].astype(BF)
            us_ref[rows, g * LANES:(g + 1) * LANES] = r[:, LANES:].astype(BF)

    _pipelined(_sub_rows(tm), project, branches)


def _full(shape):
    zeros = (0,) * len(shape)
    return pl.BlockSpec(shape, lambda *_: zeros, pipeline_mode=pl.Buffered(1))


def _layer_block(arr, l):
    idx = (l,) + (0,) * (arr.ndim - 1)
    return pl.BlockSpec((None,) + arr.shape[1:], lambda *_: idx, pipeline_mode=pl.Buffered(1))


def _mod_specs(l):
    return [pl.BlockSpec((None, None, N_MOD, D_MODEL), lambda b, t: (l, b, 0, 0)),
            pl.BlockSpec((None, None, N_MOD, D_MODEL), lambda b, t: (l, BATCH, 0, 0))]


def _proj(l, x_parts, mods, rope, wa, wk, wuvt, wuq, erep, e_val_t, cs128, kvg, qg, gqg, gkg, onem, oneg):
    tm = TM_ALL
    lay = functools.partial(_layer_block, l=l)
    tok = lambda w: pl.BlockSpec((None, tm, w), lambda b, t: (b, t, 0))
    tok_t = lambda w: pl.BlockSpec((None, w, tm), lambda b, t: (b, 0, t))
    outs = ((MLA_HEADS * LANES, False), (MLA_HEADS * LANES, True), (GQA_WIDTH, False),
            (GQA_KV_HEADS * LANES, True), (MLA_HEADS * LANES, False), (GQA_WIDTH, False),
            (F_WIDTH, False), (F_WIDTH, False))
    return pl.pallas_call(
        functools.partial(_proj_kernel, n_x=len(_token_specs(x_parts, tm))),
        grid=(BATCH, T_ALL // tm),
        in_specs=[
            *_token_specs(x_parts, tm),
            *_mod_specs(l),
            pl.BlockSpec((6, tm, LANES), lambda b, t: (0, t, 0)),
            lay(wa), lay(wk), lay(wuvt), lay(wuq),
            _full(erep.shape), _full(e_val_t.shape), _full(cs128.shape),
            lay(kvg), lay(qg), lay(gqg), lay(gkg),
            _full(onem.shape), _full(oneg.shape),
        ],
        out_specs=[tok_t(w) if tr else tok(w) for w, tr in outs],
        out_shape=[jax.ShapeDtypeStruct((BATCH, w, T_ALL) if tr else (BATCH, T_ALL, w), BF) for w, tr in outs],
        compiler_params=pltpu.CompilerParams(
            dimension_semantics=("parallel", "parallel"), vmem_limit_bytes=VMEM_LIMIT_BYTES),
        name="proj",
    )(*_token_operands(x_parts, tm), mods, mods, rope, wa, wk, wuvt, wuq, erep, e_val_t, cs128,
      kvg, qg, gqg, gkg, onem, oneg)


def _ctx_first(n_lat, with_ctx):
    if not with_ctx:
        return lambda t: t
    return lambda t: jnp.where(t == 0, n_lat, t - 1)


def _fourier_kernel(uc_ref, us_ref, cl_ref, sl_ref, cc_ref, sc_ref, o_ref, *, with_ctx):
    def latent():
        y = _dot(cl_ref[...], uc_ref[:SEQ, :]) + _dot(sl_ref[...], us_ref[:SEQ, :])
        o_ref[...] = (y * (SEQ * F_GROUP_DIM) ** -0.5).astype(BF)

    if not with_ctx:
        latent()
        return
    pl.when(pl.program_id(1) > 0)(latent)

    @pl.when(pl.program_id(1) == 0)
    def _():
        y = _dot(cc_ref[...], uc_ref[SEQ:, :]) + _dot(sc_ref[...], us_ref[SEQ:, :])
        o_ref[:CTX_LEN, :] = (y * (CTX_LEN * F_GROUP_DIM) ** -0.5).astype(BF)


def _fourier(uc, us, cl, sl, cc, sc, with_ctx):
    n_lat = SEQ // TR
    n_rows = T_ALL if with_ctx else SEQ
    blk = _ctx_first(n_lat, with_ctx)
    whole = pl.BlockSpec((None, T_ALL, F_WIDTH), lambda b, r: (b, 0, 0))
    tab = pl.BlockSpec((TR, SEQ), lambda b, r: (jnp.maximum(r - 1, 0) if with_ctx else r, 0))
    return pl.pallas_call(
        functools.partial(_fourier_kernel, with_ctx=with_ctx),
        grid=(BATCH, pl.cdiv(n_rows, TR)),
        in_specs=[whole, whole, tab, tab, _full(cc.shape), _full(sc.shape)],
        out_specs=pl.BlockSpec((None, TR, F_WIDTH), lambda b, r: (b, blk(r), 0)),
        out_shape=jax.ShapeDtypeStruct((BATCH, n_rows, F_WIDTH), BF),
        compiler_params=pltpu.CompilerParams(
            dimension_semantics=("parallel", "arbitrary"), vmem_limit_bytes=VMEM_LIMIT_BYTES),
        name="fourier",
    )(uc, us, cl, sl, cc, sc)


N_HEADS = 8
ROW_UNIT = 256


def _attn_units(q_ref, k_ref, vt_ref, o_ref, row_units, k_lo, n_keys, head_w, heads_per_vblock):
    heads_per_chunk = MXU_DIM // head_w
    n_rows = row_units[0].stop - row_units[0].start
    lane = lax.broadcasted_iota(jnp.int32, (n_rows, MXU_DIM), 1)
    lane_blk = lax.broadcasted_iota(jnp.int32, (n_rows, LANES), 1)
    units = [(rows, j) for rows in row_units for j in range(N_HEADS)]

    def scores(u):
        rows, j = units[u]
        ch = j // heads_per_chunk
        lo = (j % heads_per_chunk) * head_w
        qc = q_ref[rows, ch * MXU_DIM:(ch + 1) * MXU_DIM]
        qj = jnp.where((lane >= lo) & (lane < lo + head_w), qc, jnp.zeros_like(qc))
        return _dot_nt(k_ref[k_lo:k_lo + n_keys, ch * MXU_DIM:(ch + 1) * MXU_DIM], qj)

    def softmax(s):
        s = s.astype(BF)
        return jnp.exp2(s - jnp.max(s, axis=0, keepdims=True))

    even = {}

    def values(u, e):
        rows, j = units[u]
        r0 = (j // heads_per_vblock) * LANES
        ot = _dot(vt_ref[r0:r0 + LANES, k_lo:k_lo + n_keys], e)
        blk = (ot * (1.0 / ot[ONES_ROW:ONES_ROW + 1, :])).T
        if j % 2 == 0:
            even[rows.start] = blk
        else:
            pair = jnp.where(lane_blk < MLA_V_DIM, even.pop(rows.start), pltpu.roll(blk, MLA_V_DIM, 1))
            o_ref[rows, (j // 2) * LANES:(j // 2 + 1) * LANES] = pair.astype(BF)

    n = len(units)
    s_q = {u: scores(u) for u in range(min(2, n))}
    e_q = {}
    for u in range(n):
        e_q[u] = softmax(s_q.pop(u))
        if u + 2 < n:
            s_q[u + 2] = scores(u + 2)
        if u >= 1:
            values(u - 1, e_q.pop(u - 1))
    values(n - 1, e_q.pop(n - 1))


def _attn_kernel(q_ref, k_ref, vt_ref, o_ref, *, head_w, heads_per_vblock, with_ctx):
    def latent():
        units = [slice(r, r + ROW_UNIT) for r in range(0, q_ref.shape[0], ROW_UNIT)]
        _attn_units(q_ref, k_ref, vt_ref, o_ref, units, 0, T_ALL, head_w, heads_per_vblock)

    if not with_ctx:
        latent()
        return
    pl.when(pl.program_id(1) > 0)(latent)

    @pl.when(pl.program_id(1) == 0)
    def _():
        _attn_units(q_ref, k_ref, vt_ref, o_ref, [slice(0, CTX_LEN)], SEQ, CTX_LEN, head_w, heads_per_vblock)


def _attention(q, k, vt, head_w, heads_per_vblock, with_ctx, name):
    qw = N_HEADS * head_w
    ow = N_HEADS * MLA_V_DIM
    n_rows = T_ALL if with_ctx else SEQ
    tq = TQ // 2 if (with_ctx and qw > GQA_WIDTH) else TQ
    blk = _ctx_first(SEQ // tq, with_ctx)
    return pl.pallas_call(
        functools.partial(_attn_kernel, head_w=head_w, heads_per_vblock=heads_per_vblock, with_ctx=with_ctx),
        grid=(BATCH, pl.cdiv(n_rows, tq)),
        in_specs=[
            pl.BlockSpec((None, tq, qw), lambda b, t: (b, blk(t), 0)),
            pl.BlockSpec((None, T_ALL, qw), lambda b, t: (b, 0, 0)),
            pl.BlockSpec((None, vt.shape[1], T_ALL), lambda b, t: (b, 0, 0)),
        ],
        out_specs=pl.BlockSpec((None, tq, ow), lambda b, t: (b, blk(t), 0)),
        out_shape=jax.ShapeDtypeStruct((BATCH, n_rows, ow), BF),
        compiler_params=pltpu.CompilerParams(
            dimension_semantics=("parallel", "arbitrary"), vmem_limit_bytes=VMEM_LIMIT_BYTES),
        name=name,
    )(q, k, vt)


def _mixout_kernel(*refs, n_x):
    (modx_ref, modc_ref, f_ref, am_ref, ag_ref, wg_ref, bg_ref,
     wfo_ref, wmo_ref, wgo_ref, wo_ref, lng_ref, lnb_ref, o_ref) = refs[n_x:]
    tm, read_x = _token_reader(refs[:n_x], pl.program_id(1))

    def modulated(rows):
        is_ctx = _is_ctx_rows(pl.program_id(1), tm, rows)
        shift = _mod_row(is_ctx, modx_ref, modc_ref, 0)
        scale = _mod_row(is_ctx, modx_ref, modc_ref, 1)
        return (_layer_norm(read_x(rows)) * (1.0 + scale) + shift).astype(BF)

    def mix_and_norm(rows, h):
        mix = None
        for i, (a_ref, w_ref) in enumerate(((f_ref, wfo_ref), (am_ref, wmo_ref), (ag_ref, wgo_ref))):
            z = _dot(h, wg_ref[:, i * D_MODEL:(i + 1) * D_MODEL]) + bg_ref[:, i * D_MODEL:(i + 1) * D_MODEL]
            term = _dot(a_ref[rows, :], w_ref[...]) * (1.0 / (1.0 + jnp.exp(-z)))
            mix = term if mix is None else mix + term
        y = _dot(mix.astype(BF), wo_ref[...])
        gate1 = _mod_row(_is_ctx_rows(pl.program_id(1), tm, rows), modx_ref, modc_ref, 2)
        o_ref[rows, :] = _layer_norm(ALPHA * read_x(rows) + gate1 * y) * lng_ref[...] + lnb_ref[...]

    _pipelined(_sub_rows(tm), modulated, mix_and_norm)


def _token_grid(with_ctx):
    return (TM_ALL, T_ALL) if with_ctx else (TM_LAT, SEQ)


def _mixout(l, x_parts, mods, f, am, ag, wg, bg, wfo, wmo, wgo, wo, lng, lnb, with_ctx):
    tm, n_rows = _token_grid(with_ctx)
    tok = lambda w: pl.BlockSpec((None, tm, w), lambda b, t: (b, t, 0))
    lay = functools.partial(_layer_block, l=l)
    return pl.pallas_call(
        functools.partial(_mixout_kernel, n_x=len(_token_specs(x_parts, tm))),
        grid=(BATCH, n_rows // tm),
        in_specs=[
            *_token_specs(x_parts, tm),
            *_mod_specs(l),
            tok(F_WIDTH), tok(MLA_WIDTH), tok(GQA_WIDTH),
            lay(wg), lay(bg), lay(wfo), lay(wmo), lay(wgo), lay(wo), lay(lng), lay(lnb),
        ],
        out_specs=tok(D_MODEL),
        out_shape=jax.ShapeDtypeStruct((BATCH, n_rows, D_MODEL), F32),
        compiler_params=pltpu.CompilerParams(
            dimension_semantics=("parallel", "parallel"), vmem_limit_bytes=VMEM_LIMIT_BYTES),
        name="mixout",
    )(*_token_operands(x_parts, tm), mods, mods, f, am, ag, wg, bg, wfo, wmo, wgo, wo, lng, lnb)


def _mlp_kernel(x_ref, modx_ref, modc_ref, w1_ref, w2_ref, lng_ref, lnb_ref, o_ref):
    tm = x_ref.shape[0]

    def modulated(rows):
        is_ctx = _is_ctx_rows(pl.program_id(1), tm, rows)
        shift = _mod_row(is_ctx, modx_ref, modc_ref, 3)
        scale = _mod_row(is_ctx, modx_ref, modc_ref, 4)
        return (_layer_norm(x_ref[rows, :]) * (1.0 + scale) + shift).astype(BF)

    def mlp_and_norm(rows, h):
        u = jnp.maximum(_dot(h, w1_ref[...]), 0.0)
        y = _dot((u * u).astype(BF), w2_ref[...])
        gate2 = _mod_row(_is_ctx_rows(pl.program_id(1), tm, rows), modx_ref, modc_ref, 5)
        o_ref[rows, :] = _layer_norm(ALPHA * x_ref[rows, :] + gate2 * y) * lng_ref[...] + lnb_ref[...]

    _pipelined(_sub_rows(tm), modulated, mlp_and_norm)


def _mlp(l, x_all, mods, w1, w2, lng, lnb, with_ctx):
    tm, n_rows = _token_grid(with_ctx)
    tok = pl.BlockSpec((None, tm, D_MODEL), lambda b, t: (b, t, 0))
    lay = functools.partial(_layer_block, l=l)
    return pl.pallas_call(
        _mlp_kernel,
        grid=(BATCH, n_rows // tm),
        in_specs=[tok, *_mod_specs(l), lay(w1), lay(w2), lay(lng), lay(lnb)],
        out_specs=tok,
        out_shape=jax.ShapeDtypeStruct((BATCH, n_rows, D_MODEL), F32),
        compiler_params=pltpu.CompilerParams(
            dimension_semantics=("parallel", "parallel"), vmem_limit_bytes=VMEM_LIMIT_BYTES),
        name="mlp",
    )(x_all, mods, mods, w1, w2, lng, lnb)


def _pad_heads(w, n_heads, head_dim):
    lead = w.shape[:-1]
    w = w.reshape(*lead, n_heads, head_dim)
    w = jnp.pad(w, [(0, 0)] * len(lead) + [(0, 0), (0, LANES - head_dim)])
    return w.reshape(*lead, n_heads * LANES)


def kernel(x, c, ctx, c_ctx, w_ada, b_ada, w_in, b_gate, mla_q_g, mla_kv_g, w_uq, w_uk, w_uv,
           gqa_q_g, gqa_k_g, w_fo, w_mo, w_go, w_o, ln1_g, ln1_b, w1, w2, ln2_g, ln2_b):
    L = DEPTH
    cl_np, sl_np = _dft_tables(SEQ)
    cc_np, sc_np = _dft_tables(CTX_LEN)
    c128_np, s128_np = _dft_tables(F_GROUP_DIM)
    e_kr_np, e_rep_np, e_val_t_np = _placement_tables()
    cl, sl, cc_t, sc_t = (jnp.asarray(a).astype(BF) for a in (cl_np, sl_np, cc_np, sc_np))
    cs128 = jnp.asarray(np.concatenate([c128_np, -s128_np], axis=1)).astype(BF)
    e_rep = jnp.asarray(e_rep_np).astype(BF)
    e_val_t = jnp.asarray(e_val_t_np).astype(BF)
    one_m = jnp.asarray(_ones_rows(MLA_HEADS * LANES))
    one_g = jnp.asarray(_ones_rows(GQA_KV_HEADS * LANES))
    rope = jnp.asarray(_rope_tables())

    zc = lambda n: jnp.zeros((L, D_MODEL, n), w_in.dtype)
    o_kr, o_kg, o_vg = MLA_KV_RANK, MLA_KV_RANK + MLA_ROPE_DIM, MLA_KV_RANK + MLA_ROPE_DIM + GQA_KV_WIDTH
    o_f = KV_COLS
    o_cq = o_f + F_WIDTH
    o_qg = o_cq + MLA_Q_RANK
    o_gate = o_qg + GQA_WIDTH
    wa = jnp.concatenate([
        w_in[:, :, :o_kr], w_in[:, :, o_kg:o_vg], w_in[:, :, o_vg:o_f], w_in[:, :, o_f:o_cq],
        w_in[:, :, o_cq:o_qg], w_in[:, :, o_qg:o_gate],
        zc(KR_LANE), w_in[:, :, o_kr:o_kg], zc(LANES - KR_LANE - MLA_ROPE_DIM)], axis=-1).astype(BF)
    wg = w_in[:, :, o_gate:].astype(BF)
    wk = jnp.concatenate([
        _pad_heads(w_uk, MLA_HEADS, MLA_NOPE_DIM),
        jnp.broadcast_to(jnp.asarray(e_kr_np), (L,) + e_kr_np.shape)], axis=1).astype(BF)
    wuq = _pad_heads(w_uq, MLA_HEADS, MLA_QK_DIM).astype(BF)
    wuvt = jnp.swapaxes(_pad_heads(w_uv, MLA_HEADS, MLA_V_DIM), 1, 2).astype(BF)
    wfo, wmo, wgo, wo = (w.astype(BF) for w in (w_fo, w_mo, w_go, w_o))
    w1b, w2b = w1.astype(BF), w2.astype(BF)
    row = lambda a: a.reshape(L, 1, a.shape[-1])
    kvg, qg = row(mla_kv_g), row(mla_q_g)
    gqg = row(jnp.concatenate([gqa_q_g, gqa_q_g], axis=-1))
    gkg = row(jnp.concatenate([gqa_k_g, gqa_k_g], axis=-1))
    bg = row(b_gate)
    l1g, l1b, l2g, l2b = row(ln1_g), row(ln1_b), row(ln2_g), row(ln2_b)

    cc = jnp.concatenate([c, c_ctx[None, :], jnp.zeros((MOD_ROWS - BATCH - 1, D_MODEL), c.dtype)], axis=0)
    mods = _modulation(cc, w_ada, b_ada).reshape(L, MOD_ROWS, N_MOD, D_MODEL)

    x_parts = (x, ctx)
    for l in range(L):
        km, vmt, kg, vgt, qm, qgq, uc, us = _proj(
            l, x_parts, mods, rope, wa, wk, wuvt, wuq, e_rep, e_val_t, cs128, kvg, qg, gqg, gkg, one_m, one_g)
        with_ctx = l < L - 1
        f = _fourier(uc, us, cl, sl, cc_t, sc_t, with_ctx)
        am = _attention(qm, km, vmt, MLA_HEAD_PAD, 1, with_ctx, "attn_mla")
        ag = _attention(qgq, kg, vgt, GQA_HEAD_DIM, GQA_GROUP, with_ctx, "attn_gqa")
        x_all = _mixout(l, x_parts if with_ctx else x_parts[:1], mods, f, am, ag,
                        wg, bg, wfo, wmo, wgo, wo, l1g, l1b, with_ctx)
        x_all = _mlp(l, x_all, mods, w1b, w2b, l2g, l2b, with_ctx)
        x_parts = (x_all,)
    return x_all
```

```python
import functools
import math

import numpy as np
import jax
import jax.numpy as jnp
from jax import lax
from jax.experimental import pallas as pl
from jax.experimental.pallas import tpu as pltpu

D_MODEL = 1024
BATCH = 8
SEQ = 2048
DEPTH = 4
CTX_LEN = 256
GRID_W = 64
ROPE_THETA = 10000.0
EPS = 1e-6

F_GROUPS = 4
F_GROUP_DIM = 128
F_WIDTH = F_GROUPS * F_GROUP_DIM
MLA_HEADS = 8
MLA_Q_RANK = 256
MLA_KV_RANK = 256
MLA_NOPE_DIM = 64
MLA_ROPE_DIM = 32
MLA_QK_DIM = MLA_NOPE_DIM + MLA_ROPE_DIM
MLA_V_DIM = 64
MLA_WIDTH = MLA_HEADS * MLA_V_DIM
GQA_HEADS = 8
GQA_KV_HEADS = 2
GQA_GROUP = GQA_HEADS // GQA_KV_HEADS
GQA_HEAD_DIM = 64
GQA_WIDTH = GQA_HEADS * GQA_HEAD_DIM
GQA_KV_WIDTH = GQA_KV_HEADS * GQA_HEAD_DIM
N_MOD = 6
KV_COLS = MLA_KV_RANK + MLA_ROPE_DIM + 2 * GQA_KV_WIDTH

T_ALL = SEQ + CTX_LEN
ALPHA = (2.0 * DEPTH) ** 0.25
LOG2E = math.log2(math.e)

LANES = 128
MXU_DIM = 256
VMEM_LIMIT_BYTES = 56 * 1024 * 1024

MLA_HEAD_PAD = LANES
MOD_ROWS = 16

A_CKV, A_KG, A_VG, A_F, A_CQ, A_QG, A_KR = 0, 256, 384, 512, 1024, 1280, 1792
A_COLS = 1920
KR_LANE = 64

TM_ALL = 768
TM_LAT = 512
SUB_ROWS = 256
ONES_ROW = 64
TQ = 1024
TR = 1024
MOD_COLS = 2 * D_MODEL

BF = jnp.bfloat16
F32 = jnp.float32


def _dft_tables(n):
    k = np.arange(n, dtype=np.int64)
    ang = 2.0 * np.pi * ((k[:, None] * k[None, :]) % n).astype(np.float64) / n
    return np.cos(ang).astype(np.float32), (-np.sin(ang)).astype(np.float32)


def _rope_tables():
    t = np.arange(SEQ)
    rows = (t // GRID_W).astype(np.float64)
    cols = (t % GRID_W).astype(np.float64)

    def angles(d_rot):
        n = d_rot // 4
        freqs = ROPE_THETA ** (-np.arange(n, dtype=np.float64) / n)
        return np.concatenate([rows[:, None] * freqs, cols[:, None] * freqs], axis=-1)

    out = np.zeros((6, T_ALL, LANES), np.float64)
    out[0] = 1.0
    out[3] = 1.0
    a = angles(MLA_ROPE_DIM)
    h = MLA_ROPE_DIM // 2
    b0 = MLA_NOPE_DIM
    out[0, :SEQ, b0:b0 + h] = np.cos(a)
    out[0, :SEQ, b0 + h:b0 + 2 * h] = np.cos(a)
    out[1, :SEQ, b0 + h:b0 + 2 * h] = np.sin(a)
    out[2, :SEQ, b0:b0 + h] = -np.sin(a)
    a = angles(GQA_HEAD_DIM)
    h = GQA_HEAD_DIM // 2
    for b0 in (0, GQA_HEAD_DIM):
        out[3, :SEQ, b0:b0 + h] = np.cos(a)
        out[3, :SEQ, b0 + h:b0 + 2 * h] = np.cos(a)
        out[4, :SEQ, b0 + h:b0 + 2 * h] = np.sin(a)
        out[5, :SEQ, b0:b0 + h] = -np.sin(a)
    return out.astype(np.float32)


def _placement_tables():
    e_kr = np.zeros((LANES, MLA_HEADS * MLA_HEAD_PAD), np.float32)
    for hh in range(MLA_HEADS):
        for d in range(MLA_ROPE_DIM):
            e_kr[KR_LANE + d, hh * MLA_HEAD_PAD + MLA_NOPE_DIM + d] = 1.0
    e_rep = np.zeros((GQA_KV_WIDTH, GQA_WIDTH), np.float32)
    for g in range(GQA_KV_HEADS):
        for j in range(GQA_GROUP):
            for d in range(GQA_HEAD_DIM):
                e_rep[g * GQA_HEAD_DIM + d, (g * GQA_GROUP + j) * GQA_HEAD_DIM + d] = 1.0
    e_val_t = np.zeros((GQA_KV_HEADS * LANES, GQA_KV_WIDTH), np.float32)
    for g in range(GQA_KV_HEADS):
        for d in range(GQA_HEAD_DIM):
            e_val_t[g * LANES + d, g * GQA_HEAD_DIM + d] = 1.0
    return e_kr, e_rep, e_val_t


def _ones_rows(n_rows):
    v = np.zeros((n_rows, SUB_ROWS), np.float32)
    v[ONES_ROW::LANES, :] = 1.0
    return v


def _layer_norm(x):
    mu = jnp.mean(x, axis=-1, keepdims=True)
    xc = x - mu
    var = jnp.mean(xc * xc, axis=-1, keepdims=True)
    return xc * lax.rsqrt(var + EPS)


def _rms(x):
    return x * lax.rsqrt(jnp.mean(x * x, axis=-1, keepdims=True) + EPS)


def _sub_rows(tm):
    return [slice(r, r + SUB_ROWS) for r in range(0, tm, SUB_ROWS)]


def _pipelined(subs, first, second):
    nxt = first(subs[0])
    for i, rows in enumerate(subs):
        cur = nxt
        if i + 1 < len(subs):
            nxt = first(subs[i + 1])
        second(rows, cur)


def _is_ctx_rows(tile_idx, tm, rows):
    return tile_idx * tm + rows.start >= SEQ


def _token_reader(x_refs, tile_idx):
    if len(x_refs) == 1:
        return x_refs[0].shape[0], lambda rows: x_refs[0][rows, :]
    *lat, ctx = x_refs
    tm = len(lat) * SUB_ROWS

    def read(rows):
        v = lat[rows.start // SUB_ROWS][...]
        if rows.stop == tm:
            v = jnp.where(_is_ctx_rows(tile_idx, tm, rows), ctx[...], v)
        return v

    return tm, read


def _token_specs(x_parts, tm):
    if len(x_parts) == 1:
        return [pl.BlockSpec((None, tm, D_MODEL), lambda b, t: (b, t, 0))]
    assert CTX_LEN == SUB_ROWS and T_ALL % tm == 0
    n_sub, last = tm // SUB_ROWS, SEQ // SUB_ROWS - 1
    lat = [pl.BlockSpec((None, SUB_ROWS, D_MODEL), lambda b, t, i=i: (b, jnp.minimum(t * n_sub + i, last), 0))
           for i in range(n_sub)]
    return lat + [pl.BlockSpec((None, CTX_LEN, D_MODEL), lambda b, t: (b, 0, 0))]


def _token_operands(x_parts, tm):
    if len(x_parts) == 1:
        return list(x_parts)
    x, ctx = x_parts
    return [x] * (tm // SUB_ROWS) + [ctx]


def _mod_row(is_ctx, modx_ref, modc_ref, i):
    return jnp.where(is_ctx, modc_ref[i:i + 1, :], modx_ref[i:i + 1, :])


def _dot(a, b):
    return jnp.dot(a, b, preferred_element_type=F32)


def _rope(x, cos, sin_fwd, sin_bwd, half):
    return x * cos + pltpu.roll(x, half, 1) * sin_fwd + pltpu.roll(x, LANES - half, 1) * sin_bwd


def _head_rms_pair(x, gain):
    lane = lax.broadcasted_iota(jnp.int32, x.shape, 1)
    first = lane < GQA_HEAD_DIM
    sq = x * x
    s0 = jnp.sum(jnp.where(first, sq, 0.0), axis=-1, keepdims=True)
    s1 = jnp.sum(jnp.where(first, 0.0, sq), axis=-1, keepdims=True)
    ms = jnp.where(first, s0, s1) * (1.0 / GQA_HEAD_DIM)
    return x * lax.rsqrt(ms + EPS) * gain


def _split_bf16(a):
    hi = a.astype(BF)
    return hi, (a - hi.astype(F32)).astype(BF)


def _mod_kernel(cc_ref, w_ref, b_ref, o_ref):
    cc = cc_ref[...]
    s = cc * (1.0 / (1.0 + jnp.exp(-cc)))
    s_hi, s_lo = _split_bf16(s)
    w_hi, w_lo = _split_bf16(w_ref[...])
    acc = _dot(jnp.concatenate([s_hi, s_lo], axis=0), w_hi)
    o_ref[...] = acc[:MOD_ROWS] + acc[MOD_ROWS:] + _dot(s_hi, w_lo) + b_ref[...]


def _modulation(cc, w_ada, b_ada):
    return pl.pallas_call(
        _mod_kernel,
        grid=(DEPTH, N_MOD * D_MODEL // MOD_COLS),
        in_specs=[
            pl.BlockSpec((MOD_ROWS, D_MODEL), lambda l, j: (0, 0)),
            pl.BlockSpec((None, D_MODEL, MOD_COLS), lambda l, j: (l, 0, j)),
            pl.BlockSpec((None, 1, MOD_COLS), lambda l, j: (l, 0, j)),
        ],
        out_specs=pl.BlockSpec((None, MOD_ROWS, MOD_COLS), lambda l, j: (l, 0, j)),
        out_shape=jax.ShapeDtypeStruct((DEPTH, MOD_ROWS, N_MOD * D_MODEL), F32),
        compiler_params=pltpu.CompilerParams(
            dimension_semantics=("arbitrary", "arbitrary"), vmem_limit_bytes=VMEM_LIMIT_BYTES),
        name="modulation",
    )(cc, w_ada, b_ada.reshape(DEPTH, 1, N_MOD * D_MODEL))


def _dot_nt(a, b):
    return lax.dot_general(a, b, (((1,), (1,)), ((), ())), preferred_element_type=F32)


def _proj_kernel(*refs, n_x):
    (modx_ref, modc_ref, rope_ref, wa_ref, wk_ref, wuvt_ref, wuq_ref,
     erep_ref, evalt_ref, cs_ref, kvg_ref, qg_ref, gqg_ref, gkg_ref, onem_ref, oneg_ref,
     km_ref, vmt_ref, kg_ref, vgt_ref, qm_ref, qgo_ref, uc_ref, us_ref) = refs[n_x:]
    tm, read_x = _token_reader(refs[:n_x], pl.program_id(1))

    def project(rows):
        is_ctx = _is_ctx_rows(pl.program_id(1), tm, rows)
        shift = _mod_row(is_ctx, modx_ref, modc_ref, 0)
        scale = _mod_row(is_ctx, modx_ref, modc_ref, 1)
        h = (_layer_norm(read_x(rows)) * (1.0 + scale) + shift).astype(BF)
        return _dot(h, wa_ref[...])

    def branches(rows, p):
        cos_m, sf_m, sb_m = rope_ref[0, rows, :], rope_ref[1, rows, :], rope_ref[2, rows, :]
        cos_g, sf_g, sb_g = rope_ref[3, rows, :], rope_ref[4, rows, :], rope_ref[5, rows, :]

        ckv = (_rms(p[:, A_CKV:A_CKV + MLA_KV_RANK]) * kvg_ref[...]).astype(BF)
        kr = _rope(p[:, A_KR:A_KR + LANES], cos_m, sf_m, sb_m, MLA_ROPE_DIM // 2).astype(BF)
        km_ref[rows, :] = _dot(jnp.concatenate([ckv, kr], axis=-1), wk_ref[...]).astype(BF)
        vmt_ref[:, rows] = (_dot_nt(wuvt_ref[...], ckv) + onem_ref[...]).astype(BF)

        kg = _rope(_head_rms_pair(p[:, A_KG:A_KG + LANES], gkg_ref[...]), cos_g, sf_g, sb_g, GQA_HEAD_DIM // 2)
        kg_ref[rows, :] = _dot(kg.astype(BF), erep_ref[...]).astype(BF)
        vgt_ref[:, rows] = (_dot_nt(evalt_ref[...], p[:, A_VG:A_VG + LANES].astype(BF)) + oneg_ref[...]).astype(BF)

        cq = (_rms(p[:, A_CQ:A_CQ + MLA_Q_RANK]) * qg_ref[...]).astype(BF)
        qm = _dot(cq, wuq_ref[...])
        sm = MLA_QK_DIM ** -0.5 * LOG2E
        for hh in range(MLA_HEADS):
            blk = qm[:, hh * LANES:(hh + 1) * LANES]
            qm_ref[rows, hh * LANES:(hh + 1) * LANES] = (
                _rope(blk, cos_m, sf_m, sb_m, MLA_ROPE_DIM // 2) * sm).astype(BF)

        sg = GQA_HEAD_DIM ** -0.5 * LOG2E
        for bb in range(GQA_WIDTH // LANES):
            blk = p[:, A_QG + bb * LANES:A_QG + (bb + 1) * LANES]
            blk = _rope(_head_rms_pair(blk, gqg_ref[...]), cos_g, sf_g, sb_g, GQA_HEAD_DIM // 2)
            qgo_ref[rows, bb * LANES:(bb + 1) * LANES] = (blk * sg).astype(BF)

        for g in range(F_GROUPS):
            fg = p[:, A_F + g * LANES:A_F + (g + 1) * LANES].astype(BF)
            r = _dot(fg, cs_ref[...])
            uc_ref[rows, g * LANES:(g + 1) * LANES] = r[:, :LANES].astype(BF)
            us_ref[rows, g * LANES:(g + 1) * LANES] = r[:, LANES:].astype(BF)

    _pipelined(_sub_rows(tm), project, branches)


def _full(shape):
    zeros = (0,) * len(shape)
    return pl.BlockSpec(shape, lambda *_: zeros, pipeline_mode=pl.Buffered(1))


def _layer_block(arr, l):
    idx = (l,) + (0,) * (arr.ndim - 1)
    return pl.BlockSpec((None,) + arr.shape[1:], lambda *_: idx, pipeline_mode=pl.Buffered(1))


def _mod_specs(l):
    return [pl.BlockSpec((None, None, N_MOD, D_MODEL), lambda b, t: (l, b, 0, 0)),
            pl.BlockSpec((None, None, N_MOD, D_MODEL), lambda b, t: (l, BATCH, 0, 0))]


def _proj(l, x_parts, mods, rope, wa, wk, wuvt, wuq, erep, e_val_t, cs128, kvg, qg, gqg, gkg, onem, oneg):
    tm = TM_ALL
    lay = functools.partial(_layer_block, l=l)
    tok = lambda w: pl.BlockSpec((None, tm, w), lambda b, t: (b, t, 0))
    tok_t = lambda w: pl.BlockSpec((None, w, tm), lambda b, t: (b, 0, t))
    outs = ((MLA_HEADS * LANES, False), (MLA_HEADS * LANES, True), (GQA_WIDTH, False),
            (GQA_KV_HEADS * LANES, True), (MLA_HEADS * LANES, False), (GQA_WIDTH, False),
            (F_WIDTH, False), (F_WIDTH, False))
    return pl.pallas_call(
        functools.partial(_proj_kernel, n_x=len(_token_specs(x_parts, tm))),
        grid=(BATCH, T_ALL // tm),
        in_specs=[
            *_token_specs(x_parts, tm),
            *_mod_specs(l),
            pl.BlockSpec((6, tm, LANES), lambda b, t: (0, t, 0)),
            lay(wa), lay(wk), lay(wuvt), lay(wuq),
            _full(erep.shape), _full(e_val_t.shape), _full(cs128.shape),
            lay(kvg), lay(qg), lay(gqg), lay(gkg),
            _full(onem.shape), _full(oneg.shape),
        ],
        out_specs=[tok_t(w) if tr else tok(w) for w, tr in outs],
        out_shape=[jax.ShapeDtypeStruct((BATCH, w, T_ALL) if tr else (BATCH, T_ALL, w), BF) for w, tr in outs],
        compiler_params=pltpu.CompilerParams(
            dimension_semantics=("parallel", "parallel"), vmem_limit_bytes=VMEM_LIMIT_BYTES),
        name="proj",
    )(*_token_operands(x_parts, tm), mods, mods, rope, wa, wk, wuvt, wuq, erep, e_val_t, cs128,
      kvg, qg, gqg, gkg, onem, oneg)


def _ctx_first(n_lat, with_ctx):
    if not with_ctx:
        return lambda t: t
    return lambda t: jnp.where(t == 0, n_lat, t - 1)


def _fourier_kernel(uc_ref, us_ref, cl_ref, sl_ref, cc_ref, sc_ref, o_ref, *, with_ctx):
    def latent():
        y = _dot(cl_ref[...], uc_ref[:SEQ, :]) + _dot(sl_ref[...], us_ref[:SEQ, :])
        o_ref[...] = (y * (SEQ * F_GROUP_DIM) ** -0.5).astype(BF)

    if not with_ctx:
        latent()
        return
    pl.when(pl.program_id(1) > 0)(latent)

    @pl.when(pl.program_id(1) == 0)
    def _():
        y = _dot(cc_ref[...], uc_ref[SEQ:, :]) + _dot(sc_ref[...], us_ref[SEQ:, :])
        o_ref[:CTX_LEN, :] = (y * (CTX_LEN * F_GROUP_DIM) ** -0.5).astype(BF)


def _fourier(uc, us, cl, sl, cc, sc, with_ctx):
    n_lat = SEQ // TR
    n_rows = T_ALL if with_ctx else SEQ
    blk = _ctx_first(n_lat, with_ctx)
    whole = pl.BlockSpec((None, T_ALL, F_WIDTH), lambda b, r: (b, 0, 0))
    tab = pl.BlockSpec((TR, SEQ), lambda b, r: (jnp.maximum(r - 1, 0) if with_ctx else r, 0))
    return pl.pallas_call(
        functools.partial(_fourier_kernel, with_ctx=with_ctx),
        grid=(BATCH, pl.cdiv(n_rows, TR)),
        in_specs=[whole, whole, tab, tab, _full(cc.shape), _full(sc.shape)],
        out_specs=pl.BlockSpec((None, TR, F_WIDTH), lambda b, r: (b, blk(r), 0)),
        out_shape=jax.ShapeDtypeStruct((BATCH, n_rows, F_WIDTH), BF),
        compiler_params=pltpu.CompilerParams(
            dimension_semantics=("parallel", "arbitrary"), vmem_limit_bytes=VMEM_LIMIT_BYTES),
        name="fourier",
    )(uc, us, cl, sl, cc, sc)


N_HEADS = 8
ROW_UNIT = 256


def _attn_units(q_ref, k_ref, vt_ref, o_ref, row_units, k_lo, n_keys, head_w, heads_per_vblock):
    heads_per_chunk = MXU_DIM // head_w
    n_rows = row_units[0].stop - row_units[0].start
    lane = lax.broadcasted_iota(jnp.int32, (n_rows, MXU_DIM), 1)
    lane_blk = lax.broadcasted_iota(jnp.int32, (n_rows, LANES), 1)
    units = [(rows, j) for rows in row_units for j in range(N_HEADS)]

    def scores(u):
        rows, j = units[u]
        ch = j // heads_per_chunk
        lo = (j % heads_per_chunk) * head_w
        qc = q_ref[rows, ch * MXU_DIM:(ch + 1) * MXU_DIM]
        qj = jnp.where((lane >= lo) & (lane < lo + head_w), qc, jnp.zeros_like(qc))
        return _dot_nt(k_ref[k_lo:k_lo + n_keys, ch * MXU_DIM:(ch + 1) * MXU_DIM], qj)

    def softmax(s):
        s = s.astype(BF)
        return jnp.exp2(s - jnp.max(s, axis=0, keepdims=True))

    even = {}

    def values(u, e):
        rows, j = units[u]
        r0 = (j // heads_per_vblock) * LANES
        ot = _dot(vt_ref[r0:r0 + LANES, k_lo:k_lo + n_keys], e)
        blk = (ot * (1.0 / ot[ONES_ROW:ONES_ROW + 1, :])).T
        if j % 2 == 0:
            even[rows.start] = blk
        else:
            pair = jnp.where(lane_blk < MLA_V_DIM, even.pop(rows.start), pltpu.roll(blk, MLA_V_DIM, 1))
            o_ref[rows, (j // 2) * LANES:(j // 2 + 1) * LANES] = pair.astype(BF)

    n = len(units)
    s_q = {u: scores(u) for u in range(min(2, n))}
    e_q = {}
    for u in range(n):
        e_q[u] = softmax(s_q.pop(u))
        if u + 2 < n:
            s_q[u + 2] = scores(u + 2)
        if u >= 1:
            values(u - 1, e_q.pop(u - 1))
    values(n - 1, e_q.pop(n - 1))


def _attn_kernel(q_ref, k_ref, vt_ref, o_ref, *, head_w, heads_per_vblock, with_ctx):
    def latent():
        units = [slice(r, r + ROW_UNIT) for r in range(0, q_ref.shape[0], ROW_UNIT)]
        _attn_units(q_ref, k_ref, vt_ref, o_ref, units, 0, T_ALL, head_w, heads_per_vblock)

    if not with_ctx:
        latent()
        return
    pl.when(pl.program_id(1) > 0)(latent)

    @pl.when(pl.program_id(1) == 0)
    def _():
        _attn_units(q_ref, k_ref, vt_ref, o_ref, [slice(0, CTX_LEN)], SEQ, CTX_LEN, head_w, heads_per_vblock)


def _attention(q, k, vt, head_w, heads_per_vblock, with_ctx, name):
    qw = N_HEADS * head_w
    ow = N_HEADS * MLA_V_DIM
    n_rows = T_ALL if with_ctx else SEQ
    tq = TQ // 2 if (with_ctx and qw > GQA_WIDTH) else TQ
    blk = _ctx_first(SEQ // tq, with_ctx)
    return pl.pallas_call(
        functools.partial(_attn_kernel, head_w=head_w, heads_per_vblock=heads_per_vblock, with_ctx=with_ctx),
        grid=(BATCH, pl.cdiv(n_rows, tq)),
        in_specs=[
            pl.BlockSpec((None, tq, qw), lambda b, t: (b, blk(t), 0)),
            pl.BlockSpec((None, T_ALL, qw), lambda b, t: (b, 0, 0)),
            pl.BlockSpec((None, vt.shape[1], T_ALL), lambda b, t: (b, 0, 0)),
        ],
        out_specs=pl.BlockSpec((None, tq, ow), lambda b, t: (b, blk(t), 0)),
        out_shape=jax.ShapeDtypeStruct((BATCH, n_rows, ow), BF),
        compiler_params=pltpu.CompilerParams(
            dimension_semantics=("parallel", "arbitrary"), vmem_limit_bytes=VMEM_LIMIT_BYTES),
        name=name,
    )(q, k, vt)


def _mixout_kernel(*refs, n_x):
    (modx_ref, modc_ref, f_ref, am_ref, ag_ref, wg_ref, bg_ref,
     wfo_ref, wmo_ref, wgo_ref, wo_ref, lng_ref, lnb_ref, o_ref) = refs[n_x:]
    tm, read_x = _token_reader(refs[:n_x], pl.program_id(1))

    def modulated(rows):
        is_ctx = _is_ctx_rows(pl.program_id(1), tm, rows)
        shift = _mod_row(is_ctx, modx_ref, modc_ref, 0)
        scale = _mod_row(is_ctx, modx_ref, modc_ref, 1)
        return (_layer_norm(read_x(rows)) * (1.0 + scale) + shift).astype(BF)

    def mix_and_norm(rows, h):
        mix = None
        for i, (a_ref, w_ref) in enumerate(((f_ref, wfo_ref), (am_ref, wmo_ref), (ag_ref, wgo_ref))):
            z = _dot(h, wg_ref[:, i * D_MODEL:(i + 1) * D_MODEL]) + bg_ref[:, i * D_MODEL:(i + 1) * D_MODEL]
            term = _dot(a_ref[rows, :], w_ref[...]) * (1.0 / (1.0 + jnp.exp(-z)))
            mix = term if mix is None else mix + term
        y = _dot(mix.astype(BF), wo_ref[...])
        gate1 = _mod_row(_is_ctx_rows(pl.program_id(1), tm, rows), modx_ref, modc_ref, 2)
        o_ref[rows, :] = _layer_norm(ALPHA * read_x(rows) + gate1 * y) * lng_ref[...] + lnb_ref[...]

    _pipelined(_sub_rows(tm), modulated, mix_and_norm)


def _token_grid(with_ctx):
    return (TM_ALL, T_ALL) if with_ctx else (TM_LAT, SEQ)


def _mixout(l, x_parts, mods, f, am, ag, wg, bg, wfo, wmo, wgo, wo, lng, lnb, with_ctx):
    tm, n_rows = _token_grid(with_ctx)
    tok = lambda w: pl.BlockSpec((None, tm, w), lambda b, t: (b, t, 0))
    lay = functools.partial(_layer_block, l=l)
    return pl.pallas_call(
        functools.partial(_mixout_kernel, n_x=len(_token_specs(x_parts, tm))),
        grid=(BATCH, n_rows // tm),
        in_specs=[
            *_token_specs(x_parts, tm),
            *_mod_specs(l),
            tok(F_WIDTH), tok(MLA_WIDTH), tok(GQA_WIDTH),
            lay(wg), lay(bg), lay(wfo), lay(wmo), lay(wgo), lay(wo), lay(lng), lay(lnb),
        ],
        out_specs=tok(D_MODEL),
        out_shape=jax.ShapeDtypeStruct((BATCH, n_rows, D_MODEL), F32),
        compiler_params=pltpu.CompilerParams(
            dimension_semantics=("parallel", "parallel"), vmem_limit_bytes=VMEM_LIMIT_BYTES),
        name="mixout",
    )(*_token_operands(x_parts, tm), mods, mods, f, am, ag, wg, bg, wfo, wmo, wgo, wo, lng, lnb)


def _mlp_kernel(x_ref, modx_ref, modc_ref, w1_ref, w2_ref, lng_ref, lnb_ref, o_ref):
    tm = x_ref.shape[0]

    def modulated(rows):
        is_ctx = _is_ctx_rows(pl.program_id(1), tm, rows)
        shift = _mod_row(is_ctx, modx_ref, modc_ref, 3)
        scale = _mod_row(is_ctx, modx_ref, modc_ref, 4)
        return (_layer_norm(x_ref[rows, :]) * (1.0 + scale) + shift).astype(BF)

    def mlp_and_norm(rows, h):
        u = jnp.maximum(_dot(h, w1_ref[...]), 0.0)
        y = _dot((u * u).astype(BF), w2_ref[...])
        gate2 = _mod_row(_is_ctx_rows(pl.program_id(1), tm, rows), modx_ref, modc_ref, 5)
        o_ref[rows, :] = _layer_norm(ALPHA * x_ref[rows, :] + gate2 * y) * lng_ref[...] + lnb_ref[...]

    _pipelined(_sub_rows(tm), modulated, mlp_and_norm)


def _mlp(l, x_all, mods, w1, w2, lng, lnb, with_ctx):
    tm, n_rows = _token_grid(with_ctx)
    tok = pl.BlockSpec((None, tm, D_MODEL), lambda b, t: (b, t, 0))
    lay = functools.partial(_layer_block, l=l)
    return pl.pallas_call(
        _mlp_kernel,
        grid=(BATCH, n_rows // tm),
        in_specs=[tok, *_mod_specs(l), lay(w1), lay(w2), lay(lng), lay(lnb)],
        out_specs=tok,
        out_shape=jax.ShapeDtypeStruct((BATCH, n_rows, D_MODEL), F32),
        compiler_params=pltpu.CompilerParams(
            dimension_semantics=("parallel", "parallel"), vmem_limit_bytes=VMEM_LIMIT_BYTES),
        name="mlp",
    )(x_all, mods, mods, w1, w2, lng, lnb)


def _pad_heads(w, n_heads, head_dim):
    lead = w.shape[:-1]
    w = w.reshape(*lead, n_heads, head_dim)
    w = jnp.pad(w, [(0, 0)] * len(lead) + [(0, 0), (0, LANES - head_dim)])
    return w.reshape(*lead, n_heads * LANES)


def kernel(x, c, ctx, c_ctx, w_ada, b_ada, w_in, b_gate, mla_q_g, mla_kv_g, w_uq, w_uk, w_uv,
           gqa_q_g, gqa_k_g, w_fo, w_mo, w_go, w_o, ln1_g, ln1_b, w1, w2, ln2_g, ln2_b):
    L = DEPTH
    cl_np, sl_np = _dft_tables(SEQ)
    cc_np, sc_np = _dft_tables(CTX_LEN)
    c128_np, s128_np = _dft_tables(F_GROUP_DIM)
    e_kr_np, e_rep_np, e_val_t_np = _placement_tables()
    cl, sl, cc_t, sc_t = (jnp.asarray(a).astype(BF) for a in (cl_np, sl_np, cc_np, sc_np))
    cs128 = jnp.asarray(np.concatenate([c128_np, -s128_np], axis=1)).astype(BF)
    e_rep = jnp.asarray(e_rep_np).astype(BF)
    e_val_t = jnp.asarray(e_val_t_np).astype(BF)
    one_m = jnp.asarray(_ones_rows(MLA_HEADS * LANES))
    one_g = jnp.asarray(_ones_rows(GQA_KV_HEADS * LANES))
    rope = jnp.asarray(_rope_tables())

    zc = lambda n: jnp.zeros((L, D_MODEL, n), w_in.dtype)
    o_kr, o_kg, o_vg = MLA_KV_RANK, MLA_KV_RANK + MLA_ROPE_DIM, MLA_KV_RANK + MLA_ROPE_DIM + GQA_KV_WIDTH
    o_f = KV_COLS
    o_cq = o_f + F_WIDTH
    o_qg = o_cq + MLA_Q_RANK
    o_gate = o_qg + GQA_WIDTH
    wa = jnp.concatenate([
        w_in[:, :, :o_kr], w_in[:, :, o_kg:o_vg], w_in[:, :, o_vg:o_f], w_in[:, :, o_f:o_cq],
        w_in[:, :, o_cq:o_qg], w_in[:, :, o_qg:o_gate],
        zc(KR_LANE), w_in[:, :, o_kr:o_kg], zc(LANES - KR_LANE - MLA_ROPE_DIM)], axis=-1).astype(BF)
    wg = w_in[:, :, o_gate:].astype(BF)
    wk = jnp.concatenate([
        _pad_heads(w_uk, MLA_HEADS, MLA_NOPE_DIM),
        jnp.broadcast_to(jnp.asarray(e_kr_np), (L,) + e_kr_np.shape)], axis=1).astype(BF)
    wuq = _pad_heads(w_uq, MLA_HEADS, MLA_QK_DIM).astype(BF)
    wuvt = jnp.swapaxes(_pad_heads(w_uv, MLA_HEADS, MLA_V_DIM), 1, 2).astype(BF)
    wfo, wmo, wgo, wo = (w.astype(BF) for w in (w_fo, w_mo, w_go, w_o))
    w1b, w2b = w1.astype(BF), w2.astype(BF)
    row = lambda a: a.reshape(L, 1, a.shape[-1])
    kvg, qg = row(mla_kv_g), row(mla_q_g)
    gqg = row(jnp.concatenate([gqa_q_g, gqa_q_g], axis=-1))
    gkg = row(jnp.concatenate([gqa_k_g, gqa_k_g], axis=-1))
    bg = row(b_gate)
    l1g, l1b, l2g, l2b = row(ln1_g), row(ln1_b), row(ln2_g), row(ln2_b)

    cc = jnp.concatenate([c, c_ctx[None, :], jnp.zeros((MOD_ROWS - BATCH - 1, D_MODEL), c.dtype)], axis=0)
    mods = _modulation(cc, w_ada, b_ada).reshape(L, MOD_ROWS, N_MOD, D_MODEL)

    x_parts = (x, ctx)
    for l in range(L):
        km, vmt, kg, vgt, qm, qgq, uc, us = _proj(
            l, x_parts, mods, rope, wa, wk, wuvt, wuq, e_rep, e_val_t, cs128, kvg, qg, gqg, gkg, one_m, one_g)
        with_ctx = l < L - 1
        f = _fourier(uc, us, cl, sl, cc_t, sc_t, with_ctx)
        am = _attention(qm, km, vmt, MLA_HEAD_PAD, 1, with_ctx, "attn_mla")
        ag = _attention(qgq, kg, vgt, GQA_HEAD_DIM, GQA_GROUP, with_ctx, "attn_gqa")
        x_all = _mixout(l, x_parts if with_ctx else x_parts[:1], mods, f, am, ag,
                        wg, bg, wfo, wmo, wgo, wo, l1g, l1b, with_ctx)
        x_all = _mlp(l, x_all, mods, w1b, w2b, l2g, l2b, with_ctx)
        x_parts = (x_all,)
    return x_all
```
